```python
import math
import jax, jax.numpy as jnp
from jax import lax
import numpy as np

D_MODEL = 1024
BATCH = 4
SEQ = 8192
DEPTH = 4
DEC_BATCH = 8
DEC_SEQ = 64
PAST_LEN = 4096

CHUNK = 64
D_MIX = D_MODEL
D_SSM = D_MIX // 2
D_POOL = D_MIX - D_SSM
SSM_HEAD_DIM = 64
SSM_HEADS = D_SSM // SSM_HEAD_DIM
SSM_GROUPS = 2
SSM_HPG = SSM_HEADS // SSM_GROUPS
D_STATE = 128
CONV_W = 4
CONV_DIM = D_SSM + 2 * SSM_GROUPS * D_STATE
SSD_BLOCK = CHUNK
POOL_WINDOWS = (2, 4, 8, 16)
N_POOL_GROUPS = len(POOL_WINDOWS)
POOL_GROUP = D_POOL // N_POOL_GROUPS
POOL_HIST = max(POOL_WINDOWS) - 1
D_FF = ((8 * D_MODEL + 2) // 3 + 255) // 256 * 256
D_PLE = 256
IN_COLS = D_SSM + CONV_DIM + SSM_HEADS + D_POOL
EPS = 1e-6

kernel_name = "ssd_pool_hybrid_stream_step"


def _rmsnorm(x, g):
    xf = x.astype(jnp.float32)
    y = xf * lax.rsqrt(jnp.mean(xf * xf, axis=-1, keepdims=True) + EPS)
    return (y * g.astype(jnp.float32)).astype(x.dtype)


def _causal_dwconv(u, hist, w, b):
    ext = jnp.concatenate([hist.astype(u.dtype), u], axis=1)
    out = lax.conv_general_dilated(ext, w.astype(u.dtype)[:, None, :], window_strides=(1,), padding='VALID', dimension_numbers=('NWC', 'WIO', 'NWC'), feature_group_count=u.shape[-1])
    return out + b.astype(u.dtype), ext[:, ext.shape[1] - (CONV_W - 1):]


def _ssd_scan(x, dt, a, bm, cm, h0):
    f32 = jnp.float32
    nb, nl = x.shape[0], x.shape[1]
    q = SSD_BLOCK
    nc = -(-nl // q)
    pad = nc * q - nl
    x, dt, bm, cm = x.astype(f32), dt.astype(f32), bm.astype(f32), cm.astype(f32)
    if pad:
        padf = lambda t: jnp.pad(t, [(0, 0), (0, pad)] + [(0, 0)] * (t.ndim - 2))
        x, dt, bm, cm = padf(x), padf(dt), padf(bm), padf(cm)
    xc = x.reshape(nb, nc, q, SSM_GROUPS, SSM_HPG, SSM_HEAD_DIM)
    dtc = dt.reshape(nb, nc, q, SSM_GROUPS, SSM_HPG)
    bc = bm.reshape(nb, nc, q, SSM_GROUPS, D_STATE)
    cc = cm.reshape(nb, nc, q, SSM_GROUPS, D_STATE)
    acum = jnp.moveaxis(jnp.cumsum(dtc * a.reshape(SSM_GROUPS, SSM_HPG), axis=2), 2, -1)
    diff = acum[..., :, None] - acum[..., None, :]
    causal = jnp.tril(jnp.ones((q, q), dtype=bool))
    lmat = jnp.exp(jnp.where(causal, diff, -jnp.inf))
    xdt = xc * dtc[..., None]
    cb = jnp.einsum('bcign,bcjgn->bcgij', cc, bc)
    y_diag = jnp.einsum('bcgij,bcgrij,bcjgrp->bcigrp', cb, lmat, xdt)
    decay_end = jnp.exp(acum[..., -1:] - acum)
    chunk_states = jnp.einsum('bcjgn,bcgrj,bcjgrp->bcgrpn', bc, decay_end, xdt)
    chunk_decay = jnp.exp(acum[..., -1])

    def step(h, inp):
        s, d = inp
        return h * d[..., None, None] + s, h

    hg0 = h0.astype(f32).reshape(nb, SSM_GROUPS, SSM_HPG, SSM_HEAD_DIM, D_STATE)
    h_fin, h_in = lax.scan(step, hg0, (jnp.moveaxis(chunk_states, 1, 0), jnp.moveaxis(chunk_decay, 1, 0)))
    h_in = jnp.moveaxis(h_in, 0, 1)
    y_off = jnp.einsum('bcign,bcgrpn,bcgri->bcigrp', cc, h_in, jnp.exp(acum))
    y = (y_diag + y_off).reshape(nb, nc * q, SSM_HEADS, SSM_HEAD_DIM)[:, :nl]
    return y, h_fin.reshape(nb, SSM_HEADS, SSM_HEAD_DIM, D_STATE)


def _multiscale_pool(u, hist, pos0, pool_w, pool_b, pool_scale):
    nb, nl = u.shape[0], u.shape[1]
    ext = jnp.concatenate([hist.astype(u.dtype), u], axis=1)
    cs = jnp.pad(jnp.cumsum(ext.astype(jnp.float32), axis=1), ((0, 0), (1, 0), (0, 0)))
    end = cs[:, POOL_HIST + 1:]
    pos = pos0 + jnp.arange(nl)
    outs = []
    for gi, w in enumerate(POOL_WINDOWS):
        lo, hi = gi * POOL_GROUP, (gi + 1) * POOL_GROUP
        start = cs[:, POOL_HIST + 1 - w: POOL_HIST + 1 - w + nl, lo:hi]
        cnt = jnp.minimum(pos + 1, w).astype(jnp.float32)[None, :, None]
        outs.append((end[..., lo:hi] - start) / cnt)
    pooled = jnp.concatenate(outs, axis=-1) - u.astype(jnp.float32)
    pooled = pooled.reshape(nb, nl, N_POOL_GROUPS, POOL_GROUP)
    out = jnp.einsum('blgc,gcd->blgd', pooled, pool_w.astype(jnp.float32)) + pool_b.astype(jnp.float32)
    out = out.reshape(nb, nl, D_POOL) * pool_scale.astype(jnp.float32)
    return out, ext[:, ext.shape[1] - POOL_HIST:]


def _layer(h, p_i, conv_hist, ssm_h0, pool_hist, pos0, pre_mix_g, w_in, conv_w, conv_b, dt_bias, a_log, d_skip, ssm_norm_g, pool_w, pool_b, pool_scale, w_out, post_mix_g, pre_ffn_g, w_gate, w_up, w_down, post_ffn_g, w_ple_gate, w_ple_proj, ple_norm_g):
    f32 = jnp.float32
    nb, nl = h.shape[0], h.shape[1]
    hn = _rmsnorm(h, pre_mix_g)
    proj = hn @ w_in
    z, xbc, dt_raw, u_pool = jnp.split(proj, [D_SSM, D_SSM + CONV_DIM, D_SSM + CONV_DIM + SSM_HEADS], axis=-1)
    xbc, new_conv = _causal_dwconv(xbc, conv_hist, conv_w, conv_b)
    xbc = jax.nn.silu(xbc)
    xs, bm, cm = jnp.split(xbc, [D_SSM, D_SSM + SSM_GROUPS * D_STATE], axis=-1)
    xs = xs.reshape(nb, nl, SSM_HEADS, SSM_HEAD_DIM)
    bm = bm.reshape(nb, nl, SSM_GROUPS, D_STATE)
    cm = cm.reshape(nb, nl, SSM_GROUPS, D_STATE)
    dt = jax.nn.softplus(dt_raw.astype(f32) + dt_bias.astype(f32))
    a = -jnp.exp(a_log.astype(f32))
    y, new_ssm = _ssd_scan(xs, dt, a, bm, cm, ssm_h0)
    y = y + d_skip.astype(f32)[:, None] * xs.astype(f32)
    y = y.reshape(nb, nl, D_SSM) * jax.nn.silu(z.astype(f32))
    yg = y.reshape(nb, nl, SSM_GROUPS, D_SSM // SSM_GROUPS)
    yg = yg * lax.rsqrt(jnp.mean(yg * yg, axis=-1, keepdims=True) + EPS)
    y = yg.reshape(nb, nl, D_SSM) * ssm_norm_g.astype(f32)
    pool_out, new_pool = _multiscale_pool(u_pool, pool_hist, pos0, pool_w, pool_b, pool_scale)
    mix = jnp.concatenate([y.astype(h.dtype), pool_out.astype(h.dtype)], axis=-1) @ w_out
    h = h + _rmsnorm(mix, post_mix_g)
    hn = _rmsnorm(h, pre_ffn_g)
    f = (jax.nn.silu(hn @ w_gate) * (hn @ w_up)) @ w_down
    h = h + _rmsnorm(f, post_ffn_g)
    gate = jax.nn.sigmoid((h @ w_ple_gate).astype(f32))
    e = (p_i @ w_ple_proj).astype(f32) * gate
    h = h + _rmsnorm(e, ple_norm_g).astype(h.dtype)
    return h, new_conv, new_ssm, new_pool


def setup_inputs(seed: int = 0) -> dict:
    key = jax.random.key(seed)
    ks = jax.random.split(key, 40)
    f32 = jnp.float32

    def nrm(k, shape, scale):
        return jax.random.normal(k, shape, f32) * scale

    def gain(k, n):
        return 1.0 + 0.05 * jax.random.normal(k, (DEPTH, n), f32)

    dt0 = jnp.exp(jax.random.uniform(ks[10], (DEPTH, SSM_HEADS), f32) * (math.log(0.1) - math.log(0.001)) + math.log(0.001))
    dt_bias = dt0 + jnp.log(-jnp.expm1(-dt0))
    a_log = jnp.log(jax.random.uniform(ks[11], (DEPTH, SSM_HEADS), f32, 1.0, 16.0))
    return {
        'x_prompt': nrm(ks[0], (BATCH, SEQ, D_MODEL), 1.0),
        'x_sample': nrm(ks[1], (DEC_BATCH, DEC_SEQ, D_MODEL), 1.0),
        'state_ssm': nrm(ks[2], (DEPTH, DEC_BATCH, SSM_HEADS, SSM_HEAD_DIM, D_STATE), 0.5),
        'state_conv': nrm(ks[3], (DEPTH, DEC_BATCH, CONV_W - 1, CONV_DIM), 1.0),
        'state_pool': nrm(ks[4], (DEPTH, DEC_BATCH, POOL_HIST, D_POOL), 1.0),
        'p_prompt': nrm(ks[5], (DEPTH, BATCH, SEQ, D_PLE), 1.0),
        'p_sample': nrm(ks[6], (DEPTH, DEC_BATCH, DEC_SEQ, D_PLE), 1.0),
        'pre_mix_g': gain(ks[7], D_MODEL),
        'w_in': nrm(ks[8], (DEPTH, D_MODEL, IN_COLS), D_MODEL ** -0.5),
        'conv_w': nrm(ks[9], (DEPTH, CONV_W, CONV_DIM), CONV_W ** -0.5),
        'conv_b': nrm(ks[12], (DEPTH, CONV_DIM), 0.01),
        'dt_bias': dt_bias,
        'a_log': a_log,
        'd_skip': 1.0 + 0.05 * jax.random.normal(ks[13], (DEPTH, SSM_HEADS), f32),
        'ssm_norm_g': gain(ks[14], D_SSM),
        'pool_w': nrm(ks[15], (DEPTH, N_POOL_GROUPS, POOL_GROUP, POOL_GROUP), POOL_GROUP ** -0.5),
        'pool_b': nrm(ks[16], (DEPTH, N_POOL_GROUPS, POOL_GROUP), 0.01),
        'pool_scale': 1.0 + 0.05 * jax.random.normal(ks[17], (DEPTH, D_POOL), f32),
        'w_out': nrm(ks[18], (DEPTH, D_MIX, D_MODEL), D_MIX ** -0.5),
        'post_mix_g': gain(ks[19], D_MODEL),
        'pre_ffn_g': gain(ks[20], D_MODEL),
        'w_gate': nrm(ks[21], (DEPTH, D_MODEL, D_FF), D_MODEL ** -0.5),
        'w_up': nrm(ks[22], (DEPTH, D_MODEL, D_FF), D_MODEL ** -0.5),
        'w_down': nrm(ks[23], (DEPTH, D_FF, D_MODEL), D_FF ** -0.5),
        'post_ffn_g': gain(ks[24], D_MODEL),
        'w_ple_gate': nrm(ks[25], (DEPTH, D_MODEL, D_MODEL), D_MODEL ** -0.5),
        'w_ple_proj': nrm(ks[26], (DEPTH, D_PLE, D_MODEL), D_PLE ** -0.5),
        'ple_norm_g': gain(ks[27], D_MODEL),
    }


def reference(x_prompt, x_sample, state_ssm, state_conv, state_pool, p_prompt, p_sample, pre_mix_g, w_in, conv_w, conv_b, dt_bias, a_log, d_skip, ssm_norm_g, pool_w, pool_b, pool_scale, w_out, post_mix_g, pre_ffn_g, w_gate, w_up, w_down, post_ffn_g, w_ple_gate, w_ple_proj, ple_norm_g):
    nbp = x_prompt.shape[0]
    conv0 = jnp.zeros((nbp, CONV_W - 1, CONV_DIM), x_prompt.dtype)
    ssm0 = jnp.zeros((nbp, SSM_HEADS, SSM_HEAD_DIM, D_STATE), jnp.float32)
    pool0 = jnp.zeros((nbp, POOL_HIST, D_POOL), x_prompt.dtype)
    hp, hs = x_prompt, x_sample
    ssm_p, conv_p, pool_p, ssm_s, conv_s, pool_s = [], [], [], [], [], []
    for i in range(DEPTH):
        lw = (pre_mix_g[i], w_in[i], conv_w[i], conv_b[i], dt_bias[i], a_log[i], d_skip[i], ssm_norm_g[i], pool_w[i], pool_b[i], pool_scale[i], w_out[i], post_mix_g[i], pre_ffn_g[i], w_gate[i], w_up[i], w_down[i], post_ffn_g[i], w_ple_gate[i], w_ple_proj[i], ple_norm_g[i])
        hp, c, s, q = _layer(hp, p_prompt[i], conv0, ssm0, pool0, 0, *lw)
        conv_p.append(c)
        ssm_p.append(s)
        pool_p.append(q)
        hs, c, s, q = _layer(hs, p_sample[i], state_conv[i], state_ssm[i], state_pool[i], PAST_LEN, *lw)
        conv_s.append(c)
        ssm_s.append(s)
        pool_s.append(q)
    return (hp, hs, jnp.stack(ssm_p), jnp.stack(conv_p), jnp.stack(pool_p), jnp.stack(ssm_s), jnp.stack(conv_s), jnp.stack(pool_s))
```

```python
import functools

import jax
import jax.numpy as jnp
from jax import lax
from jax.experimental import pallas as pl
from jax.experimental.pallas import tpu as pltpu

D_MODEL = 1024
DEPTH = 4
PAST_LEN = 4096
D_SSM = 512
D_POOL = 512
SSM_HEAD_DIM = 64
SSM_HEADS = 8
SSM_GROUPS = 2
D_STATE = 128
CONV_W = 4
CONV_DIM = 1024
POOL_WINDOWS = (2, 4, 8, 16)
POOL_GROUP = 128
POOL_HIST = 15
D_FF = 2816
D_PLE = 256
EPS = 1e-6

Q = 64
GROUP_W = D_SSM // SSM_GROUPS
HDR = 16
PROMPT_TILE = 256
FF_BLOCKS = ((0, 1024), (1024, 2048), (2048, D_FF))
NEG = -1e30
VMEM_CAPACITY_V7X = 64 * 1024 * 1024
VMEM_LIMIT = VMEM_CAPACITY_V7X - 8 * 1024 * 1024

Z0, XBC0, U0, DT0, IN_EXT = 0, 512, 1536, 2048, 2560

(R_PRE_MIX, R_POST_MIX, R_PRE_FFN, R_POST_FFN, R_PLE, R_CONV_B, R_CONV_W0) = range(7)
(R_DT_BIAS, R_A_LOG, R_D_SKIP, R_SSM_G, R_POOL_B, R_POOL_SCALE) = range(6)

f32 = jnp.float32
bf16 = jnp.bfloat16


def _rms(x, g_row):
    ms = jnp.mean(x * x, axis=-1, keepdims=True)
    return x * lax.rsqrt(ms + EPS) * g_row


def _sigmoid(x):
    return 1.0 / (1.0 + jnp.exp(-x))


def _dot(a, b):
    return jnp.dot(a, b, preferred_element_type=f32)


def _chunk_cumsum(x):
    row = lax.broadcasted_iota(jnp.int32, x.shape, 0) % Q
    sh = 1
    while sh < Q:
        x = x + jnp.where(row >= sh, pltpu.roll(x, sh, axis=0), 0.0)
        sh *= 2
    return x


def _ssd_block(xg, dtg, ag, bg, cg, s, masks):
    eye2, causal2, blockmask = masks
    a_last = ag[Q - 1:Q, :]
    xdt = xg * dtg
    bb = bg.astype(bf16)
    cb_ = cg.astype(bf16)
    b2 = jnp.concatenate([bb, bb], axis=0)
    cb2 = lax.dot_general(cb_, b2, (((1,), (1,)), ((), ())),
                          preferred_element_type=f32)
    ys = []
    for q in range(2):
        p_col = ag[:, q * 128:(q + 1) * 128]
        p_row = jnp.sum(jnp.where(eye2, p_col, 0.0), axis=0, keepdims=True)
        lmat = jnp.exp(jnp.where(causal2, p_col - p_row, NEG))
        g = (cb2 * lmat).astype(bf16)
        xp = xdt[:, q * 128:(q + 1) * 128]
        bd = jnp.where(blockmask, jnp.concatenate([xp, xp], axis=0), 0.0).astype(bf16)
        ys.append(_dot(g, bd))
    y_diag = jnp.concatenate(ys, axis=1)
    y_off = _dot(cb_, s.astype(bf16)) * jnp.exp(ag)
    xdtd = (xdt * jnp.exp(a_last - ag)).astype(bf16)
    states = _dot(bg.T.astype(bf16), xdtd)
    s_new = s * jnp.exp(a_last) + states
    return y_diag + y_off, s_new


def _layer_kernel(*refs, nseg, seg_len, streaming):
    if streaming:
        (h_ref, p_ref, va_ref, vb_ref, w_in_ref, pool_w_ref, w_out_ref, w_gate_ref, w_up_ref,
         w_down_ref, w_pg_ref, w_pp_ref,
         h_out_ref, ssm_out_ref, conv_out_ref, pool_out_ref,
         xbc_ext, pool_ext, st_ref, y_scr) = refs
        ssm0_ref = conv0_ref = pool0_ref = None
        t = pl.program_id(1)
        last_t = pl.num_programs(1) - 1
    else:
        (h_ref, p_ref, ssm0_ref, conv0_ref, pool0_ref, va_ref, vb_ref, w_in_ref, pool_w_ref,
         w_out_ref, w_gate_ref, w_up_ref, w_down_ref, w_pg_ref, w_pp_ref,
         h_out_ref, ssm_out_ref, conv_out_ref, pool_out_ref,
         xbc_ext, pool_ext, st_ref, y_scr) = refs
    L = seg_len
    M = nseg * L

    def va(r):
        return va_ref[r:r + 1, :]

    def vb(r):
        return vb_ref[r:r + 1, :]

    if streaming:
        @pl.when(t == 0)
        def _():
            xbc_ext[:, 0:HDR, :] = jnp.zeros((nseg, HDR, CONV_DIM), f32)
            pool_ext[:, 0:HDR, :] = jnp.zeros((nseg, HDR, D_POOL), f32)
            st_ref[...] = jnp.zeros(st_ref.shape, f32)

    h = h_ref[...]
    hn = _rms(h, va(R_PRE_MIX)).astype(bf16)
    proj = _dot(hn, w_in_ref[...])
    z = proj[:, Z0:XBC0]
    u = proj[:, U0:DT0]
    dt = proj[:, DT0:IN_EXT] + vb(R_DT_BIAS)
    dt = jnp.maximum(dt, 0.0) + jnp.log1p(jnp.exp(-jnp.abs(dt)))
    a_row = -jnp.exp(vb(R_A_LOG))
    acum = _chunk_cumsum(dt * a_row)

    conv_parts = []
    for s in range(nseg):
        if not streaming:
            xbc_ext[s, HDR - (CONV_W - 1):HDR, :] = conv0_ref[s]
        xbc_ext[s, HDR:HDR + L, :] = proj[s * L:(s + 1) * L, XBC0:U0]
        acc = va(R_CONV_B)
        for k in range(CONV_W):
            lo = HDR - (CONV_W - 1) + k
            acc = acc + va(R_CONV_W0 + k) * xbc_ext[s, lo:lo + L, :]
        conv_parts.append(acc)
        conv_out_ref[s] = xbc_ext[s, HDR + L - (CONV_W - 1):HDR + L, :]
    xbc = conv_parts[0] if nseg == 1 else jnp.concatenate(conv_parts, axis=0)
    xbc = xbc * _sigmoid(xbc)
    xs = xbc[:, 0:D_SSM]

    lane = lax.broadcasted_iota(jnp.int32, (Q, 128), 1)
    rowq = lax.broadcasted_iota(jnp.int32, (Q, 128), 0)
    eye2 = rowq == (lane % Q)
    causal2 = rowq >= (lane % Q)
    row2 = lax.broadcasted_iota(jnp.int32, (2 * Q, 128), 0)
    lane2 = lax.broadcasted_iota(jnp.int32, (2 * Q, 128), 1)
    blockmask = (row2 // Q) == (lane2 // SSM_HEAD_DIM)
    masks = (eye2, causal2, blockmask)

    blocks_per_seg = L // Q
    for s in range(nseg):
        for g in range(SSM_GROUPS):
            gl = slice(g * GROUP_W, (g + 1) * GROUP_W)
            hpg = SSM_HEADS // SSM_GROUPS
            if streaming:
                state = st_ref[g]
            else:
                state = ssm0_ref[s, g * hpg:(g + 1) * hpg].reshape(GROUP_W, D_STATE).T
            for c in range(blocks_per_seg):
                r0 = s * L + c * Q
                rs = slice(r0, r0 + Q)
                bg = xbc[rs, D_SSM + g * D_STATE:D_SSM + (g + 1) * D_STATE]
                cg = xbc[rs, D_SSM + (SSM_GROUPS + g) * D_STATE:D_SSM + (SSM_GROUPS + g + 1) * D_STATE]
                yg, state = _ssd_block(xs[rs, gl], dt[rs, gl], acum[rs, gl], bg, cg, state, masks)
                y_scr[rs, gl] = yg
            if streaming:
                st_ref[g] = state

                @pl.when(t == last_t)
                def _():
                    ssm_out_ref[g * hpg:(g + 1) * hpg] = state.T.reshape(hpg, SSM_HEAD_DIM, D_STATE)
            else:
                ssm_out_ref[s, g * hpg:(g + 1) * hpg] = state.T.reshape(hpg, SSM_HEAD_DIM, D_STATE)

    y = y_scr[...] + vb(R_D_SKIP) * xs
    y = y * (z * _sigmoid(z))
    yn = []
    for g in range(SSM_GROUPS):
        yg = y[:, g * GROUP_W:(g + 1) * GROUP_W]
        yn.append(yg * lax.rsqrt(jnp.mean(yg * yg, axis=-1, keepdims=True) + EPS))
    y = jnp.concatenate(yn, axis=1) * vb(R_SSM_G)

    pooled_parts = []
    for s in range(nseg):
        if not streaming:
            pool_ext[s, HDR - POOL_HIST:HDR, :] = pool0_ref[s]
        pool_ext[s, HDR:HDR + L, :] = u[s * L:(s + 1) * L]
        if streaming:
            pos0 = t * L
        else:
            pos0 = PAST_LEN
        pos1 = pos0 + 1 + lax.broadcasted_iota(jnp.int32, (L, POOL_GROUP), 0)
        outs = []
        for gi, w in enumerate(POOL_WINDOWS):
            lo, hi = gi * POOL_GROUP, (gi + 1) * POOL_GROUP
            acc = pool_ext[s, HDR:HDR + L, lo:hi]
            for k in range(1, w):
                acc = acc + pool_ext[s, HDR - k:HDR - k + L, lo:hi]
            cnt = jnp.minimum(pos1, w).astype(f32)
            outs.append(acc / cnt)
        pooled_parts.append(jnp.concatenate(outs, axis=1) - u[s * L:(s + 1) * L])
        pool_out_ref[s] = pool_ext[s, L + HDR - POOL_HIST:L + HDR, :]
    pooled = pooled_parts[0] if nseg == 1 else jnp.concatenate(pooled_parts, axis=0)
    pooled = pooled.astype(bf16)
    pool_o = jnp.concatenate(
        [_dot(pooled[:, 0:256], pool_w_ref[0]), _dot(pooled[:, 256:512], pool_w_ref[1])], axis=1)
    pool_o = (pool_o + vb(R_POOL_B)) * vb(R_POOL_SCALE)

    if streaming:
        xbc_ext[:, 0:HDR, :] = xbc_ext[:, L:L + HDR, :]
        pool_ext[:, 0:HDR, :] = pool_ext[:, L:L + HDR, :]

    mix = _dot(jnp.concatenate([y.astype(bf16), pool_o.astype(bf16)], axis=1), w_out_ref[...])
    h = h + _rms(mix, va(R_POST_MIX))

    hn = _rms(h, va(R_PRE_FFN)).astype(bf16)
    f = None
    for lo, hi in FF_BLOCKS:
        gt = _dot(hn, w_gate_ref[:, lo:hi])
        up = _dot(hn, w_up_ref[:, lo:hi])
        act = (gt * _sigmoid(gt) * up).astype(bf16)
        part = _dot(act, w_down_ref[lo:hi, :])
        f = part if f is None else f + part
    h = h + _rms(f, va(R_POST_FFN))

    gate = _sigmoid(_dot(h.astype(bf16), w_pg_ref[...]))
    e = _dot(p_ref[...].astype(bf16), w_pp_ref[...]) * gate
    h_out_ref[...] = h + _rms(e, va(R_PLE))


def _resident(shape_tail, layer):
    nd = len(shape_tail)
    return pl.BlockSpec((None,) + tuple(shape_tail), lambda *_: (layer,) + (0,) * nd,
                        pipeline_mode=pl.Buffered(1))


def _weight_specs(layer):
    return [
        _resident((16, D_MODEL), layer),
        _resident((8, D_SSM), layer),
        _resident((D_MODEL, IN_EXT), layer),
        _resident((2, 256, 256), layer),
        _resident((D_MODEL, D_MODEL), layer),
        _resident((D_MODEL, D_FF), layer),
        _resident((D_MODEL, D_FF), layer),
        _resident((D_FF, D_MODEL), layer),
        _resident((D_MODEL, D_MODEL), layer),
        _resident((D_PLE, D_MODEL), layer),
    ]


def _scratch(nseg, seg_len):
    return [
        pltpu.VMEM((nseg, HDR + seg_len, CONV_DIM), f32),
        pltpu.VMEM((nseg, HDR + seg_len, D_POOL), f32),
        pltpu.VMEM((SSM_GROUPS, D_STATE, GROUP_W), f32),
        pltpu.VMEM((nseg * seg_len, D_SSM), f32),
    ]


def _prompt_layer(layer, h, p_all, weights):
    nb, nl, _ = h.shape
    T = PROMPT_TILE
    assert nl % T == 0 and T % Q == 0
    grid = (nb, nl // T)
    in_specs = [
        pl.BlockSpec((None, T, D_MODEL), lambda s, t: (s, t, 0)),
        pl.BlockSpec((None, None, T, D_PLE), lambda s, t: (layer, s, t, 0)),
    ] + _weight_specs(layer)
    out_shape = (
        jax.ShapeDtypeStruct((nb, nl, D_MODEL), f32),
        jax.ShapeDtypeStruct((nb, SSM_HEADS, SSM_HEAD_DIM, D_STATE), f32),
        jax.ShapeDtypeStruct((nb, 1, CONV_W - 1, CONV_DIM), f32),
        jax.ShapeDtypeStruct((nb, 1, POOL_HIST, D_POOL), f32),
    )
    out_specs = (
        pl.BlockSpec((None, T, D_MODEL), lambda s, t: (s, t, 0)),
        pl.BlockSpec((None, SSM_HEADS, SSM_HEAD_DIM, D_STATE), lambda s, t: (s, 0, 0, 0)),
        pl.BlockSpec((None, 1, CONV_W - 1, CONV_DIM), lambda s, t: (s, 0, 0, 0)),
        pl.BlockSpec((None, 1, POOL_HIST, D_POOL), lambda s, t: (s, 0, 0, 0)),
    )
    h_new, ssm, conv, pool = pl.pallas_call(
        functools.partial(_layer_kernel, nseg=1, seg_len=T, streaming=True),
        grid=grid, in_specs=in_specs, out_specs=out_specs, out_shape=out_shape,
        scratch_shapes=_scratch(1, T),
        compiler_params=pltpu.CompilerParams(
            dimension_semantics=("arbitrary", "arbitrary"), vmem_limit_bytes=VMEM_LIMIT),
        name=f"prompt_layer{layer}",
    )(h, p_all, *weights)
    return h_new, ssm, conv[:, 0], pool[:, 0]


def _sample_layer(layer, h2d, p_all, ssm_all, conv_all, pool_all, weights, nb, nl):
    assert nl % Q == 0
    M = nb * nl
    full = lambda *tail: pl.BlockSpec((None,) + tail, lambda i: (layer,) + (0,) * len(tail))
    in_specs = [
        pl.BlockSpec((M, D_MODEL), lambda i: (0, 0)),
        full(M, D_PLE),
        full(nb, SSM_HEADS, SSM_HEAD_DIM, D_STATE),
        full(nb, CONV_W - 1, CONV_DIM),
        full(nb, POOL_HIST, D_POOL),
    ] + _weight_specs(layer)
    out_shape = (
        jax.ShapeDtypeStruct((M, D_MODEL), f32),
        jax.ShapeDtypeStruct((nb, SSM_HEADS, SSM_HEAD_DIM, D_STATE), f32),
        jax.ShapeDtypeStruct((nb, CONV_W - 1, CONV_DIM), f32),
        jax.ShapeDtypeStruct((nb, POOL_HIST, D_POOL), f32),
    )
    out_specs = tuple(pl.BlockSpec(s.shape, lambda i, n=len(s.shape): (0,) * n) for s in out_shape)
    return pl.pallas_call(
        functools.partial(_layer_kernel, nseg=nb, seg_len=nl, streaming=False),
        grid=(1,), in_specs=in_specs, out_specs=out_specs, out_shape=out_shape,
        scratch_shapes=_scratch(nb, nl),
        compiler_params=pltpu.CompilerParams(
            dimension_semantics=("arbitrary",), vmem_limit_bytes=VMEM_LIMIT),
        name=f"sample_layer{layer}",
    )(h2d, p_all, ssm_all, conv_all, pool_all, *weights)


def _pack_weights(pre_mix_g, w_in, conv_w, conv_b, dt_bias, a_log, d_skip, ssm_norm_g, pool_w, pool_b,
                  pool_scale, w_out, post_mix_g, pre_ffn_g, w_gate, w_up, w_down, post_ffn_g, w_ple_gate,
                  w_ple_proj, ple_norm_g):
    dt_cols = D_SSM + CONV_DIM
    w_in_ext = jnp.concatenate([
        w_in[:, :, 0:dt_cols],
        w_in[:, :, dt_cols + SSM_HEADS:],
        jnp.repeat(w_in[:, :, dt_cols:dt_cols + SSM_HEADS], SSM_HEAD_DIM, axis=2),
    ], axis=2).astype(bf16)
    rep = lambda v: jnp.repeat(v, SSM_HEAD_DIM, axis=1)
    va = jnp.stack([pre_mix_g, post_mix_g, pre_ffn_g, post_ffn_g, ple_norm_g, conv_b]
                   + [conv_w[:, k] for k in range(CONV_W)], axis=1)
    va = jnp.pad(va, ((0, 0), (0, 16 - va.shape[1]), (0, 0)))
    vb = jnp.stack([rep(dt_bias), rep(a_log), rep(d_skip), ssm_norm_g,
                    pool_b.reshape(DEPTH, D_POOL), pool_scale], axis=1)
    vb = jnp.pad(vb, ((0, 0), (0, 8 - vb.shape[1]), (0, 0)))
    zeros = jnp.zeros((DEPTH, POOL_GROUP, POOL_GROUP), f32)
    pw = [jnp.concatenate([jnp.concatenate([pool_w[:, 2 * k], zeros], axis=2),
                           jnp.concatenate([zeros, pool_w[:, 2 * k + 1]], axis=2)], axis=1)
          for k in range(2)]
    pool_w2 = jnp.stack(pw, axis=1).astype(bf16)
    return (va, vb, w_in_ext, pool_w2, w_out.astype(bf16), w_gate.astype(bf16), w_up.astype(bf16),
            w_down.astype(bf16), w_ple_gate.astype(bf16), w_ple_proj.astype(bf16))


def kernel(x_prompt, x_sample, state_ssm, state_conv, state_pool, p_prompt, p_sample, pre_mix_g, w_in, conv_w, conv_b, dt_bias, a_log, d_skip, ssm_norm_g, pool_w, pool_b, pool_scale, w_out, post_mix_g, pre_ffn_g, w_gate, w_up, w_down, post_ffn_g, w_ple_gate, w_ple_proj, ple_norm_g):
    weights = _pack_weights(pre_mix_g, w_in, conv_w, conv_b, dt_bias, a_log, d_skip, ssm_norm_g, pool_w,
                            pool_b, pool_scale, w_out, post_mix_g, pre_ffn_g, w_gate, w_up, w_down,
                            post_ffn_g, w_ple_gate, w_ple_proj, ple_norm_g)
    nbs, nls, _ = x_sample.shape
    hp = x_prompt
    hs = x_sample.reshape(nbs * nls, D_MODEL)
    p_sample2d = p_sample.reshape(DEPTH, nbs * nls, D_PLE)
    outs_p, outs_s = [], []
    for i in range(DEPTH):
        hp, ssm, conv, pool = _prompt_layer(i, hp, p_prompt, weights)
        outs_p.append((ssm, conv, pool))
        hs, ssm, conv, pool = _sample_layer(i, hs, p_sample2d, state_ssm, state_conv, state_pool,
                                            weights, nbs, nls)
        outs_s.append((ssm, conv, pool))
    stack = lambda outs, j: jnp.stack([o[j] for o in outs])
    return (hp, hs.reshape(nbs, nls, D_MODEL),
            stack(outs_p, 0), stack(outs_p, 1), stack(outs_p, 2),
            stack(outs_s, 0), stack(outs_s, 1), stack(outs_s, 2))
```

```python
import functools

import jax
import jax.numpy as jnp
from jax import lax
from jax.experimental import pallas as pl
from jax.experimental.pallas import tpu as pltpu

D_MODEL = 1024
DEPTH = 4
PAST_LEN = 4096
D_SSM = 512
D_POOL = 512
SSM_HEAD_DIM = 64
SSM_HEADS = 8
SSM_GROUPS = 2
HPG = SSM_HEADS // SSM_GROUPS
D_STATE = 128
CONV_W = 4
CONV_DIM = 1024
POOL_WINDOWS = (2, 4, 8, 16)
POOL_GROUP = 128
POOL_HIST = 15
D_FF = 2816
D_PLE = 256
EPS = 1e-6

Q = 64
GROUP_W = D_SSM // SSM_GROUPS
HDR = 16
PROMPT_TILE = 256
FF_BLOCKS = ((0, 1024), (1024, 2048), (2048, D_FF))
NEG = -1e30
MATRIX_TO_VECTOR_TIME = 1.5
VMEM_CAPACITY_V7X = 64 * 1024 * 1024
VMEM_LIMIT = VMEM_CAPACITY_V7X - 8 * 1024 * 1024

Z0, XBC0, U0, DT0, IN_EXT = 0, 512, 1536, 2048, 2560

(R_PRE_MIX, R_POST_MIX, R_PRE_FFN, R_POST_FFN, R_PLE, R_CONV_B, R_CONV_W0) = range(7)
(R_DT_BIAS, R_A_LOG, R_D_SKIP, R_SSM_G, R_POOL_B, R_POOL_SCALE) = range(6)

f32 = jnp.float32
bf16 = jnp.bfloat16


def _rms(x, g_row):
    ms = jnp.mean(x * x, axis=-1, keepdims=True)
    return x * lax.rsqrt(ms + EPS) * g_row


def _sigmoid(x):
    return 1.0 / (1.0 + jnp.exp(-x))


def _dot(a, b):
    return jnp.dot(a, b, preferred_element_type=f32)


def _chunk_cumsum(x):
    row = lax.broadcasted_iota(jnp.int32, x.shape, 0) % Q
    sh = 1
    while sh < Q:
        x = x + jnp.where(row >= sh, pltpu.roll(x, sh, axis=0), 0.0)
        sh *= 2
    return x


def _ssd_block(xg, dtg, ag, bg, cg, s, masks):
    eye2, causal2, blockmask = masks
    a_last = ag[Q - 1:Q, :]
    xdt = xg * dtg
    bb = bg.astype(bf16)
    cb_ = cg.astype(bf16)
    b2 = jnp.concatenate([bb, bb], axis=0)
    cb2 = lax.dot_general(cb_, b2, (((1,), (1,)), ((), ())),
                          preferred_element_type=f32)
    ys = []
    for q in range(2):
        p_col = ag[:, q * 128:(q + 1) * 128]
        p_row = jnp.sum(jnp.where(eye2, p_col, 0.0), axis=0, keepdims=True)
        lmat = jnp.exp(jnp.where(causal2, p_col - p_row, NEG))
        g = (cb2 * lmat).astype(bf16)
        xp = xdt[:, q * 128:(q + 1) * 128]
        bd = jnp.where(blockmask, jnp.concatenate([xp, xp], axis=0), 0.0).astype(bf16)
        ys.append(_dot(g, bd))
    y_diag = jnp.concatenate(ys, axis=1)
    y_off = _dot(cb_, s.astype(bf16)) * jnp.exp(ag)
    xdtd = (xdt * jnp.exp(a_last - ag)).astype(bf16)
    states = _dot(bg.T.astype(bf16), xdtd)
    s_new = s * jnp.exp(a_last) + states
    return y_diag + y_off, s_new


def _mixer_steps(out, h_ref, state_in, va, vb, w_in_ref, pool_w_ref, w_out_ref, state_out, scratch,
                 *, nseg, seg_len, t):
    streaming = t is not None
    ssm0_ref, conv0_ref, pool0_ref = state_in
    ssm_out_ref, conv_out_ref, pool_out_ref = state_out
    xbc_ext, pool_ext, st_ref, y_scr = scratch
    L = seg_len
    k = max(1, nseg * L // 256)
    W = GROUP_W
    seg_rows = [slice(s * L, (s + 1) * L) for s in range(nseg)]

    yield 'v', 320 * k
    hn = _rms(h_ref[...], va(R_PRE_MIX)).astype(bf16)
    if not streaming:
        for s in range(nseg):
            xbc_ext[s, HDR - (CONV_W - 1):HDR, :] = conv0_ref[s]
            pool_ext[s, HDR - POOL_HIST:HDR, :] = pool0_ref[s]

    dts = []
    for g in range(SSM_GROUPS):
        yield 'x', 256 * k
        dts.append(_dot(hn, w_in_ref[:, DT0 + g * W:DT0 + (g + 1) * W]))
    for j in range(CONV_DIM // W):
        yield 'x', 256 * k
        piece = _dot(hn, w_in_ref[:, XBC0 + j * W:XBC0 + (j + 1) * W])
        for s in range(nseg):
            xbc_ext[s, HDR:HDR + L, j * W:(j + 1) * W] = piece[seg_rows[s]]
    for s in range(nseg):
        conv_out_ref[s] = xbc_ext[s, HDR + L - (CONV_W - 1):HDR + L, :]

    acums = []
    for g in range(SSM_GROUPS):
        yield 'v', 200 * k
        d = dts[g] + vb(R_DT_BIAS)[:, g * W:(g + 1) * W]
        dts[g] = jnp.maximum(d, 0.0) + jnp.log1p(jnp.exp(-jnp.abs(d)))
        yield 'v', 300 * k
        acums.append(_chunk_cumsum(dts[g] * -jnp.exp(vb(R_A_LOG)[:, g * W:(g + 1) * W])))

    xbc = []
    for j in range(CONV_DIM // W):
        yield 'v', 300 * k
        ls = slice(j * W, (j + 1) * W)
        parts = []
        for s in range(nseg):
            acc = va(R_CONV_B)[:, ls]
            for kk in range(CONV_W):
                lo = HDR - (CONV_W - 1) + kk
                acc = acc + va(R_CONV_W0 + kk)[:, ls] * xbc_ext[s, lo:lo + L, ls]
            parts.append(acc)
        piece = parts[0] if nseg == 1 else jnp.concatenate(parts, axis=0)
        xbc.append(piece * _sigmoid(piece))
    xs, bmat, cmat = xbc[0:SSM_GROUPS], xbc[SSM_GROUPS], xbc[SSM_GROUPS + 1]
    if streaming:
        xbc_ext[:, 0:HDR, :] = xbc_ext[:, L:L + HDR, :]

    zs = []
    for g in range(SSM_GROUPS):
        yield 'x', 256 * k
        zs.append(_dot(hn, w_in_ref[:, Z0 + g * W:Z0 + (g + 1) * W]))
    for j in range(D_POOL // W):
        yield 'x', 256 * k
        piece = _dot(hn, w_in_ref[:, U0 + j * W:U0 + (j + 1) * W])
        for s in range(nseg):
            pool_ext[s, HDR:HDR + L, j * W:(j + 1) * W] = piece[seg_rows[s]]
    for s in range(nseg):
        pool_out_ref[s] = pool_ext[s, L + HDR - POOL_HIST:L + HDR, :]

    lane = lax.broadcasted_iota(jnp.int32, (Q, 128), 1)
    rowq = lax.broadcasted_iota(jnp.int32, (Q, 128), 0)
    eye2 = rowq == (lane % Q)
    causal2 = rowq >= (lane % Q)
    row2 = lax.broadcasted_iota(jnp.int32, (2 * Q, 128), 0)
    lane2 = lax.broadcasted_iota(jnp.int32, (2 * Q, 128), 1)
    blockmask = (row2 // Q) == (lane2 // SSM_HEAD_DIM)
    masks = (eye2, causal2, blockmask)

    for s in range(nseg):
        states = []
        for g in range(SSM_GROUPS):
            if streaming:
                states.append(st_ref[g])
            else:
                states.append(ssm0_ref[s, g * HPG:(g + 1) * HPG].reshape(W, D_STATE).T)
        for c in range(L // Q):
            rs = slice(s * L + c * Q, s * L + (c + 1) * Q)
            for g in range(SSM_GROUPS):
                yield 'v', 350
                bg = bmat[rs, g * D_STATE:(g + 1) * D_STATE]
                cg = cmat[rs, g * D_STATE:(g + 1) * D_STATE]
                yg, states[g] = _ssd_block(xs[g][rs], dts[g][rs], acums[g][rs], bg, cg, states[g], masks)
                y_scr[rs, g * W:(g + 1) * W] = yg
        for g in range(SSM_GROUPS):
            if streaming:
                st_ref[g] = states[g]
            else:
                ssm_out_ref[s, g * HPG:(g + 1) * HPG] = states[g].T.reshape(HPG, SSM_HEAD_DIM, D_STATE)

    ys = []
    for g in range(SSM_GROUPS):
        yield 'v', 250 * k
        gl = slice(g * W, (g + 1) * W)
        yg = y_scr[:, gl] + vb(R_D_SKIP)[:, gl] * xs[g]
        yg = yg * (zs[g] * _sigmoid(zs[g]))
        yg = yg * lax.rsqrt(jnp.mean(yg * yg, axis=-1, keepdims=True) + EPS)
        ys.append((yg * vb(R_SSM_G)[:, gl]).astype(bf16))

    pos0 = t * L if streaming else PAST_LEN
    pos1 = pos0 + 1 + lax.broadcasted_iota(jnp.int32, (L, POOL_GROUP), 0)
    pooled = []
    for gi, w in enumerate(POOL_WINDOWS):
        yield 'v', 60 * w * k
        ls = slice(gi * POOL_GROUP, (gi + 1) * POOL_GROUP)
        cnt = jnp.minimum(pos1, w).astype(f32)
        parts = []
        for s in range(nseg):
            acc = pool_ext[s, HDR:HDR + L, ls]
            for kk in range(1, w):
                acc = acc + pool_ext[s, HDR - kk:HDR - kk + L, ls]
            parts.append(acc / cnt - pool_ext[s, HDR:HDR + L, ls])
        piece = parts[0] if nseg == 1 else jnp.concatenate(parts, axis=0)
        pooled.append(piece.astype(bf16))
    if streaming:
        pool_ext[:, 0:HDR, :] = pool_ext[:, L:L + HDR, :]

    yield 'x', 128 * k
    pool_o = jnp.concatenate([_dot(jnp.concatenate(pooled[0:2], axis=1), pool_w_ref[0]),
                              _dot(jnp.concatenate(pooled[2:4], axis=1), pool_w_ref[1])], axis=1)
    pool_o = ((pool_o + vb(R_POOL_B)) * vb(R_POOL_SCALE)).astype(bf16)
    mixer_out = jnp.concatenate(ys + [pool_o], axis=1)
    mix = []
    for j in range(D_MODEL // 256):
        yield 'x', 256 * k
        mix.append(_dot(mixer_out, w_out_ref[:, j * 256:(j + 1) * 256]))
    yield 'v', 600 * k
    h1 = h_ref[...] + _rms(jnp.concatenate(mix, axis=1), va(R_POST_MIX))
    out['h1'] = h1
    out['hn1'] = _rms(h1, va(R_PRE_FFN)).astype(bf16)


def _ffn_ple_steps(out, h1_ref, hn1_ref, p_ref, va, w_gate_ref, w_up_ref, w_down_ref, w_pg_ref, w_pp_ref,
                   scratch):
    act_scr, f_scr, gate_scr = scratch
    k = max(1, h1_ref.shape[0] // 256)
    hn = hn1_ref[...]
    nout = D_MODEL // 256
    for bi, (lo, hi) in enumerate(FF_BLOCKS):
        for c0 in range(lo, hi, 256):
            yield 'x', 256 * k
            gt = _dot(hn, w_gate_ref[:, c0:c0 + 256])
            yield 'x', 256 * k
            up = _dot(hn, w_up_ref[:, c0:c0 + 256])
            yield 'v', 130 * k
            act_scr[:, c0 - lo:c0 - lo + 256] = (gt * _sigmoid(gt) * up).astype(bf16)
        for j in range(nout):
            yield 'x', (hi - lo) // 4 * k
            cols = slice(j * 256, (j + 1) * 256)
            part = _dot(act_scr[:, 0:hi - lo], w_down_ref[lo:hi, cols])
            f_scr[:, cols] = part if bi == 0 else f_scr[:, cols] + part
    yield 'v', 500 * k
    h = h1_ref[...] + _rms(f_scr[...], va(R_POST_FFN))
    hb = h.astype(bf16)
    for j in range(nout):
        yield 'x', 256 * k
        gate_scr[:, j * 256:(j + 1) * 256] = _dot(hb, w_pg_ref[:, j * 256:(j + 1) * 256])
    yield 'x', 256 * k
    e = _dot(p_ref[...].astype(bf16), w_pp_ref[...])
    yield 'v', 600 * k
    e = e * _sigmoid(gate_scr[...])
    out['h'] = h + _rms(e, va(R_PLE))


def _merge(lead, fill=None):
    spent = {'x': 0.0, 'v': 0.0}

    def advance(stream, head):
        spent[head[0]] += head[1]
        return next(stream, None)

    lead_head = next(lead, None)
    fill_head = next(fill, None) if fill is not None else None
    while lead_head is not None:
        lead_head = advance(lead, lead_head)
        while fill_head is not None and spent['x'] < MATRIX_TO_VECTOR_TIME * spent['v']:
            fill_head = advance(fill, fill_head)
    while fill_head is not None:
        fill_head = advance(fill, fill_head)


def _row_getters(va_ref, vb_ref):
    return (lambda r: va_ref[r:r + 1, :]), (lambda r: vb_ref[r:r + 1, :])


def _prompt_kernel(h_ref, p_ref, va_ref, vb_ref, w_in_ref, pool_w_ref, w_out_ref, w_gate_ref, w_up_ref,
                   w_down_ref, w_pg_ref, w_pp_ref,
                   h_out_ref, ssm_out_ref, conv_out_ref, pool_out_ref,
                   xbc_ext, pool_ext, st_ref, y_scr, h1_scr, hn1_scr, act_scr, f_scr, gate_scr,
                   *, tiles_per_seq, tile):
    n = pl.program_id(0)
    t = n % tiles_per_seq
    va, vb = _row_getters(va_ref, vb_ref)

    @pl.when(n == 0)
    def _():
        h1_scr[...] = jnp.zeros(h1_scr.shape, f32)
        hn1_scr[...] = jnp.zeros(hn1_scr.shape, bf16)

    @pl.when(t == 0)
    def _():
        xbc_ext[:, 0:HDR, :] = jnp.zeros((1, HDR, CONV_DIM), f32)
        pool_ext[:, 0:HDR, :] = jnp.zeros((1, HDR, D_POOL), f32)
        st_ref[...] = jnp.zeros(st_ref.shape, f32)

    out = {}
    mixer = _mixer_steps(out, h_ref, (None, None, None), va, vb, w_in_ref, pool_w_ref, w_out_ref,
                         (None, conv_out_ref, pool_out_ref), (xbc_ext, pool_ext, st_ref, y_scr),
                         nseg=1, seg_len=tile, t=t)
    ffn = _ffn_ple_steps(out, h1_scr, hn1_scr, p_ref, va, w_gate_ref, w_up_ref, w_down_ref, w_pg_ref,
                         w_pp_ref, (act_scr, f_scr, gate_scr))
    _merge(mixer, ffn)
    h_out_ref[...] = out['h']
    h1_scr[...] = out['h1']
    hn1_scr[...] = out['hn1']

    @pl.when(t == tiles_per_seq - 1)
    def _():
        for g in range(SSM_GROUPS):
            ssm_out_ref[g * HPG:(g + 1) * HPG] = st_ref[g].T.reshape(HPG, SSM_HEAD_DIM, D_STATE)


def _sample_kernel(h_ref, p_ref, ssm0_ref, conv0_ref, pool0_ref, va_ref, vb_ref, w_in_ref, pool_w_ref,
                   w_out_ref, w_gate_ref, w_up_ref, w_down_ref, w_pg_ref, w_pp_ref,
                   h_out_ref, ssm_out_ref, conv_out_ref, pool_out_ref,
                   xbc_ext, pool_ext, st_ref, y_scr, h1_scr, hn1_scr, act_scr, f_scr, gate_scr,
                   *, nseg, seg_len):
    va, vb = _row_getters(va_ref, vb_ref)
    out = {}
    _merge(_mixer_steps(out, h_ref, (ssm0_ref, conv0_ref, pool0_ref), va, vb, w_in_ref, pool_w_ref,
                        w_out_ref, (ssm_out_ref, conv_out_ref, pool_out_ref),
                        (xbc_ext, pool_ext, st_ref, y_scr), nseg=nseg, seg_len=seg_len, t=None))
    h1_scr[...] = out['h1']
    hn1_scr[...] = out['hn1']
    _merge(_ffn_ple_steps(out, h1_scr, hn1_scr, p_ref, va, w_gate_ref, w_up_ref, w_down_ref,
                          w_pg_ref, w_pp_ref, (act_scr, f_scr, gate_scr)))
    h_out_ref[...] = out['h']


def _resident(shape_tail, layer):
    nd = len(shape_tail)
    return pl.BlockSpec((None,) + tuple(shape_tail), lambda *_: (layer,) + (0,) * nd,
                        pipeline_mode=pl.Buffered(1))


def _weight_specs(layer):
    return [
        _resident((16, D_MODEL), layer),
        _resident((8, D_SSM), layer),
        _resident((D_MODEL, IN_EXT), layer),
        _resident((2, 256, 256), layer),
        _resident((D_MODEL, D_MODEL), layer),
        _resident((D_MODEL, D_FF), layer),
        _resident((D_MODEL, D_FF), layer),
        _resident((D_FF, D_MODEL), layer),
        _resident((D_MODEL, D_MODEL), layer),
        _resident((D_PLE, D_MODEL), layer),
    ]


def _scratch(nseg, seg_len):
    m = nseg * seg_len
    return [
        pltpu.VMEM((nseg, HDR + seg_len, CONV_DIM), f32),
        pltpu.VMEM((nseg, HDR + seg_len, D_POOL), f32),
        pltpu.VMEM((SSM_GROUPS, D_STATE, GROUP_W), f32),
        pltpu.VMEM((m, D_SSM), f32),
        pltpu.VMEM((m, D_MODEL), f32),
        pltpu.VMEM((m, D_MODEL), bf16),
        pltpu.VMEM((m, max(hi - lo for lo, hi in FF_BLOCKS)), bf16),
        pltpu.VMEM((m, D_MODEL), f32),
        pltpu.VMEM((m, D_MODEL), f32),
    ]


def _prompt_layer(layer, h, p_all, weights):
    nb, nl, _ = h.shape
    T = PROMPT_TILE
    assert nl % T == 0 and T % Q == 0
    nt = nl // T
    ntiles = nb * nt

    def cur(n):
        m = jnp.minimum(n, ntiles - 1)
        return m // nt, m % nt

    def prev(n):
        m = jnp.maximum(n - 1, 0)
        return m // nt, m % nt

    in_specs = [
        pl.BlockSpec((None, T, D_MODEL), lambda n: (*cur(n), 0)),
        pl.BlockSpec((None, None, T, D_PLE), lambda n: (layer, *prev(n), 0)),
    ] + _weight_specs(layer)
    out_shape = (
        jax.ShapeDtypeStruct((nb, nl, D_MODEL), f32),
        jax.ShapeDtypeStruct((nb, SSM_HEADS, SSM_HEAD_DIM, D_STATE), f32),
        jax.ShapeDtypeStruct((nb, 1, CONV_W - 1, CONV_DIM), f32),
        jax.ShapeDtypeStruct((nb, 1, POOL_HIST, D_POOL), f32),
    )
    out_specs = (
        pl.BlockSpec((None, T, D_MODEL), lambda n: (*prev(n), 0)),
        pl.BlockSpec((None, SSM_HEADS, SSM_HEAD_DIM, D_STATE), lambda n: (cur(n)[0], 0, 0, 0)),
        pl.BlockSpec((None, 1, CONV_W - 1, CONV_DIM), lambda n: (cur(n)[0], 0, 0, 0)),
        pl.BlockSpec((None, 1, POOL_HIST, D_POOL), lambda n: (cur(n)[0], 0, 0, 0)),
    )
    h_new, ssm, conv, pool = pl.pallas_call(
        functools.partial(_prompt_kernel, tiles_per_seq=nt, tile=T),
        grid=(ntiles + 1,), in_specs=in_specs, out_specs=out_specs, out_shape=out_shape,
        scratch_shapes=_scratch(1, T),
        compiler_params=pltpu.CompilerParams(
            dimension_semantics=("arbitrary",), vmem_limit_bytes=VMEM_LIMIT),
        name=f"prompt_layer{layer}",
    )(h, p_all, *weights)
    return h_new, ssm, conv[:, 0], pool[:, 0]


def _sample_layer(layer, h2d, p_all, ssm_all, conv_all, pool_all, weights, nb, nl):
    assert nl % Q == 0
    M = nb * nl
    full = lambda *tail: pl.BlockSpec((None,) + tail, lambda i: (layer,) + (0,) * len(tail))
    in_specs = [
        pl.BlockSpec((M, D_MODEL), lambda i: (0, 0)),
        full(M, D_PLE),
        full(nb, SSM_HEADS, SSM_HEAD_DIM, D_STATE),
        full(nb, CONV_W - 1, CONV_DIM),
        full(nb, POOL_HIST, D_POOL),
    ] + _weight_specs(layer)
    out_shape = (
        jax.ShapeDtypeStruct((M, D_MODEL), f32),
        jax.ShapeDtypeStruct((nb, SSM_HEADS, SSM_HEAD_DIM, D_STATE), f32),
        jax.ShapeDtypeStruct((nb, CONV_W - 1, CONV_DIM), f32),
        jax.ShapeDtypeStruct((nb, POOL_HIST, D_POOL), f32),
    )
    out_specs = tuple(pl.BlockSpec(s.shape, lambda i, n=len(s.shape): (0,) * n) for s in out_shape)
    return pl.pallas_call(
        functools.partial(_sample_kernel, nseg=nb, seg_len=nl),
        grid=(1,), in_specs=in_specs, out_specs=out_specs, out_shape=out_shape,
        scratch_shapes=_scratch(nb, nl),
        compiler_params=pltpu.CompilerParams(
            dimension_semantics=("arbitrary",), vmem_limit_bytes=VMEM_LIMIT),
        name=f"sample_layer{layer}",
    )(h2d, p_all, ssm_all, conv_all, pool_all, *weights)


def _pack_weights(pre_mix_g, w_in, conv_w, conv_b, dt_bias, a_log, d_skip, ssm_norm_g, pool_w, pool_b,
                  pool_scale, w_out, post_mix_g, pre_ffn_g, w_gate, w_up, w_down, post_ffn_g, w_ple_gate,
                  w_ple_proj, ple_norm_g):
    dt_cols = D_SSM + CONV_DIM
    w_in_ext = jnp.concatenate([
        w_in[:, :, 0:dt_cols],
        w_in[:, :, dt_cols + SSM_HEADS:],
        jnp.repeat(w_in[:, :, dt_cols:dt_cols + SSM_HEADS], SSM_HEAD_DIM, axis=2),
    ], axis=2).astype(bf16)
    rep = lambda v: jnp.repeat(v, SSM_HEAD_DIM, axis=1)
    va = jnp.stack([pre_mix_g, post_mix_g, pre_ffn_g, post_ffn_g, ple_norm_g, conv_b]
                   + [conv_w[:, k] for k in range(CONV_W)], axis=1)
    va = jnp.pad(va, ((0, 0), (0, 16 - va.shape[1]), (0, 0)))
    vb = jnp.stack([rep(dt_bias), rep(a_log), rep(d_skip), ssm_norm_g,
                    pool_b.reshape(DEPTH, D_POOL), pool_scale], axis=1)
    vb = jnp.pad(vb, ((0, 0), (0, 8 - vb.shape[1]), (0, 0)))
    zeros = jnp.zeros((DEPTH, POOL_GROUP, POOL_GROUP), f32)
    pw = [jnp.concatenate([jnp.concatenate([pool_w[:, 2 * k], zeros], axis=2),
                           jnp.concatenate([zeros, pool_w[:, 2 * k + 1]], axis=2)], axis=1)
          for k in range(2)]
    pool_w2 = jnp.stack(pw, axis=1).astype(bf16)
    return (va, vb, w_in_ext, pool_w2, w_out.astype(bf16), w_gate.astype(bf16), w_up.astype(bf16),
            w_down.astype(bf16), w_ple_gate.astype(bf16), w_ple_proj.astype(bf16))


def kernel(x_prompt, x_sample, state_ssm, state_conv, state_pool, p_prompt, p_sample, pre_mix_g, w_in, conv_w, conv_b, dt_bias, a_log, d_skip, ssm_norm_g, pool_w, pool_b, pool_scale, w_out, post_mix_g, pre_ffn_g, w_gate, w_up, w_down, post_ffn_g, w_ple_gate, w_ple_proj, ple_norm_g):
    weights = _pack_weights(pre_mix_g, w_in, conv_w, conv_b, dt_bias, a_log, d_skip, ssm_norm_g, pool_w,
                            pool_b, pool_scale, w_out, post_mix_g, pre_ffn_g, w_gate, w_up, w_down,
                            post_ffn_g, w_ple_gate, w_ple_proj, ple_norm_g)
    nbs, nls, _ = x_sample.shape
    hp = x_prompt
    hs = x_sample.reshape(nbs * nls, D_MODEL)
    p_sample2d = p_sample.reshape(DEPTH, nbs * nls, D_PLE)
    outs_p, outs_s = [], []
    for i in range(DEPTH):
        hp, ssm, conv, pool = _prompt_layer(i, hp, p_prompt, weights)
        outs_p.append((ssm, conv, pool))
        hs, ssm, conv, pool = _sample_layer(i, hs, p_sample2d, state_ssm, state_conv, state_pool,
                                            weights, nbs, nls)
        outs_s.append((ssm, conv, pool))
    stack = lambda outs, j: jnp.stack([o[j] for o in outs])
    return (hp, hs.reshape(nbs, nls, D_MODEL),
            stack(outs_p, 0), stack(outs_p, 1), stack(outs_p, 2),
            stack(outs_s, 0), stack(outs_s, 1), stack(outs_s, 2))
```

```python
import functools

import jax
import jax.numpy as jnp
from jax import lax
from jax.experimental import pallas as pl
from jax.experimental.pallas import tpu as pltpu

D_MODEL = 1024
DEPTH = 4
PAST_LEN = 4096
D_SSM = 512
D_POOL = 512
SSM_HEAD_DIM = 64
SSM_HEADS = 8
SSM_GROUPS = 2
HPG = SSM_HEADS // SSM_GROUPS
D_STATE = 128
CONV_W = 4
CONV_DIM = 1024
POOL_WINDOWS = (2, 4, 8, 16)
POOL_GROUP = 128
POOL_HIST = 15
D_FF = 2816
D_PLE = 256
EPS = 1e-6

Q = 64
GROUP_W = D_SSM // SSM_GROUPS
HDR = 16
PROMPT_TILE = 256
FF_BLOCKS = ((0, 1024), (1024, 2048), (2048, D_FF))
NEG = -1e30
LATE = 'late'
ROW_PIECE = 128
MATRIX_TO_VECTOR_TIME = 1.6
VMEM_CAPACITY_V7X = 64 * 1024 * 1024
VMEM_LIMIT = VMEM_CAPACITY_V7X - 8 * 1024 * 1024

Z0, XBC0, U0, DT0, IN_EXT = 0, 512, 1536, 2048, 2560

(R_PRE_MIX, R_POST_MIX, R_PRE_FFN, R_POST_FFN, R_PLE, R_CONV_B, R_CONV_W0) = range(7)
(R_DT_BIAS, R_A_LOG, R_D_SKIP, R_SSM_G, R_POOL_B, R_POOL_SCALE) = range(6)

f32 = jnp.float32
bf16 = jnp.bfloat16


def _rms(x, g_row):
    ms = jnp.mean(x * x, axis=-1, keepdims=True)
    return x * lax.rsqrt(ms + EPS) * g_row


def _sigmoid(x):
    return 1.0 / (1.0 + jnp.exp(-x))


def _dot(a, b):
    return jnp.dot(a, b, preferred_element_type=f32)


def _chunk_cumsum(x):
    row = lax.broadcasted_iota(jnp.int32, x.shape, 0) % Q
    sh = 1
    while sh < Q:
        x = x + jnp.where(row >= sh, pltpu.roll(x, sh, axis=0), 0.0)
        sh *= 2
    return x


def _ssd_block(xg, dtg, ag, bg, cg, s, masks):
    eye2, causal2, blockmask = masks
    a_last = ag[Q - 1:Q, :]
    xdt = xg * dtg
    bb = bg.astype(bf16)
    cb_ = cg.astype(bf16)
    b2 = jnp.concatenate([bb, bb], axis=0)
    cb2 = lax.dot_general(cb_, b2, (((1,), (1,)), ((), ())),
                          preferred_element_type=f32)
    ys = []
    for q in range(2):
        p_col = ag[:, q * 128:(q + 1) * 128]
        p_row = jnp.sum(jnp.where(eye2, p_col, 0.0), axis=0, keepdims=True)
        lmat = jnp.exp(jnp.where(causal2, p_col - p_row, NEG))
        g = (cb2 * lmat).astype(bf16)
        xp = xdt[:, q * 128:(q + 1) * 128]
        bd = jnp.where(blockmask, jnp.concatenate([xp, xp], axis=0), 0.0).astype(bf16)
        ys.append(_dot(g, bd))
    y_diag = jnp.concatenate(ys, axis=1)
    y_off = _dot(cb_, s.astype(bf16)) * jnp.exp(ag)
    xdtd = (xdt * jnp.exp(a_last - ag)).astype(bf16)
    states = _dot(bg.T.astype(bf16), xdtd)
    s_new = s * jnp.exp(a_last) + states
    return y_diag + y_off, s_new


def _mixer_steps(out, h_ref, state_in, va, vb, w_in_ref, pool_w_ref, w_out_ref, state_out, scratch,
                 *, nseg, seg_len, t, prenorm=None):
    streaming = t is not None
    ssm0_ref, conv0_ref, pool0_ref = state_in
    ssm_out_ref, conv_out_ref, pool_out_ref = state_out
    xbc_ext, pool_ext, st_ref, y_scr = scratch
    L = seg_len
    k = max(1, nseg * L // 256)
    W = GROUP_W
    seg_rows = [slice(s * L, (s + 1) * L) for s in range(nseg)]
    row_pieces = [slice(r, r + ROW_PIECE) for r in range(0, nseg * L, ROW_PIECE)]
    H = W // 2

    if prenorm is None:
        yield 'v', 320 * k
        hn_val = _rms(h_ref[...], va(R_PRE_MIX)).astype(bf16)
        hn = lambda: hn_val
    else:
        hn = lambda: prenorm[0][...]
    if not streaming:
        for s in range(nseg):
            xbc_ext[s, HDR - (CONV_W - 1):HDR, :] = conv0_ref[s]
            pool_ext[s, 0:HDR - POOL_HIST, :] = jnp.zeros((HDR - POOL_HIST, D_POOL), f32)
            pool_ext[s, HDR - POOL_HIST:HDR, :] = pool0_ref[s]

    dts = []
    for g in range(SSM_GROUPS):
        yield 'x', 256 * k
        dts.append(_dot(hn(), w_in_ref[:, DT0 + g * W:DT0 + (g + 1) * W]))
    for j in range(CONV_DIM // W):
        yield 'x', 256 * k
        piece = _dot(hn(), w_in_ref[:, XBC0 + j * W:XBC0 + (j + 1) * W])
        for s in range(nseg):
            xbc_ext[s, HDR:HDR + L, j * W:(j + 1) * W] = piece[seg_rows[s]]
    for s in range(nseg):
        conv_out_ref[s] = xbc_ext[s, HDR + L - (CONV_W - 1):HDR + L, :]

    acums = []
    for g in range(SSM_GROUPS):
        dt_halves, acum_halves = [], []
        for l0 in range(g * W, (g + 1) * W, H):
            yield 'v', 170 * k
            d = dts[g][:, l0 - g * W:l0 - g * W + H] + vb(R_DT_BIAS)[:, l0:l0 + H]
            d = jnp.maximum(d, 0.0) + jnp.log(1.0 + jnp.exp(-jnp.abs(d)))
            dt_halves.append(d)
            acum_halves.append(_chunk_cumsum(d * -jnp.exp(vb(R_A_LOG)[:, l0:l0 + H])))
        dts[g] = jnp.concatenate(dt_halves, axis=1)
        acums.append(jnp.concatenate(acum_halves, axis=1))

    xbc = []
    for j in range(CONV_DIM // H):
        yield 'v', 200 * k
        ls = slice(j * H, (j + 1) * H)
        parts = []
        for s in range(nseg):
            acc = va(R_CONV_B)[:, ls]
            for kk in range(CONV_W):
                lo = HDR - (CONV_W - 1) + kk
                acc = acc + va(R_CONV_W0 + kk)[:, ls] * xbc_ext[s, lo:lo + L, ls]
            parts.append(acc)
        piece = parts[0] if nseg == 1 else jnp.concatenate(parts, axis=0)
        xbc.append(piece * _sigmoid(piece))
    xbc = [jnp.concatenate(xbc[j:j + W // H], axis=1) for j in range(0, len(xbc), W // H)]
    xs, bmat, cmat = xbc[0:SSM_GROUPS], xbc[SSM_GROUPS], xbc[SSM_GROUPS + 1]
    if streaming:
        xbc_ext[:, 0:HDR, :] = xbc_ext[:, L:L + HDR, :]

    zs = []
    for g in range(SSM_GROUPS):
        yield 'x', 256 * k
        zs.append(_dot(hn(), w_in_ref[:, Z0 + g * W:Z0 + (g + 1) * W]))
    for j in range(D_POOL // W):
        yield 'x', 256 * k
        piece = _dot(hn(), w_in_ref[:, U0 + j * W:U0 + (j + 1) * W])
        for s in range(nseg):
            pool_ext[s, HDR:HDR + L, j * W:(j + 1) * W] = piece[seg_rows[s]]
    for s in range(nseg):
        pool_out_ref[s] = pool_ext[s, L + HDR - POOL_HIST:L + HDR, :]
    if prenorm is not None:
        for rows in row_pieces:
            yield 'v', 160
            prenorm[0][rows] = _rms(prenorm[1][rows], va(R_PRE_MIX)).astype(bf16)

    lane = lax.broadcasted_iota(jnp.int32, (Q, 128), 1)
    rowq = lax.broadcasted_iota(jnp.int32, (Q, 128), 0)
    eye2 = rowq == (lane % Q)
    causal2 = rowq >= (lane % Q)
    row2 = lax.broadcasted_iota(jnp.int32, (2 * Q, 128), 0)
    lane2 = lax.broadcasted_iota(jnp.int32, (2 * Q, 128), 1)
    blockmask = (row2 // Q) == (lane2 // SSM_HEAD_DIM)
    masks = (eye2, causal2, blockmask)

    for s in range(nseg):
        states = []
        for g in range(SSM_GROUPS):
            if streaming:
                states.append(st_ref[g])
            else:
                states.append(ssm0_ref[s, g * HPG:(g + 1) * HPG].reshape(W, D_STATE).T)
        for c in range(L // Q):
            rs = slice(s * L + c * Q, s * L + (c + 1) * Q)
            for g in range(SSM_GROUPS):
                yield 'v', 350
                bg = bmat[rs, g * D_STATE:(g + 1) * D_STATE]
                cg = cmat[rs, g * D_STATE:(g + 1) * D_STATE]
                yg, states[g] = _ssd_block(xs[g][rs], dts[g][rs], acums[g][rs], bg, cg, states[g], masks)
                y_scr[rs, g * W:(g + 1) * W] = yg
        for g in range(SSM_GROUPS):
            if streaming:
                st_ref[g] = states[g]
            else:
                ssm_out_ref[s, g * HPG:(g + 1) * HPG] = states[g].T.reshape(HPG, SSM_HEAD_DIM, D_STATE)

    ys = []
    for g in range(SSM_GROUPS):
        gl = slice(g * W, (g + 1) * W)
        parts = []
        for rows in row_pieces:
            yield 'v', 130
            yg = y_scr[rows, gl] + vb(R_D_SKIP)[:, gl] * xs[g][rows]
            yg = yg * (zs[g][rows] * _sigmoid(zs[g][rows]))
            yg = yg * lax.rsqrt(jnp.mean(yg * yg, axis=-1, keepdims=True) + EPS)
            parts.append((yg * vb(R_SSM_G)[:, gl]).astype(bf16))
        ys.append(jnp.concatenate(parts, axis=0))

    pos0 = t * L if streaming else PAST_LEN
    pos1 = pos0 + 1 + lax.broadcasted_iota(jnp.int32, (L, POOL_GROUP), 0)
    assert POOL_WINDOWS == tuple(2 ** (i + 1) for i in range(len(POOL_WINDOWS))) and HDR > POOL_HIST
    pooled = []
    sums = None
    for gi, w in enumerate(POOL_WINDOWS):
        yield 'v', (150 - 30 * gi) * k
        if sums is None:
            sums = [pool_ext[s] for s in range(nseg)]
        sums = [x + pltpu.roll(x, w // 2, axis=0) for x in sums]
        cnt = jnp.minimum(pos1, w).astype(f32)
        ls = slice(gi * POOL_GROUP, (gi + 1) * POOL_GROUP)
        parts = [x[HDR:HDR + L, 0:POOL_GROUP] / cnt - pool_ext[s, HDR:HDR + L, ls]
                 for s, x in enumerate(sums)]
        piece = parts[0] if nseg == 1 else jnp.concatenate(parts, axis=0)
        pooled.append(piece.astype(bf16))
        sums = [x[:, POOL_GROUP:] for x in sums]
    if streaming:
        pool_ext[:, 0:HDR, :] = pool_ext[:, L:L + HDR, :]

    yield 'x', 128 * k
    pool_o = jnp.concatenate([_dot(jnp.concatenate(pooled[0:2], axis=1), pool_w_ref[0]),
                              _dot(jnp.concatenate(pooled[2:4], axis=1), pool_w_ref[1])], axis=1)
    pool_o = ((pool_o + vb(R_POOL_B)) * vb(R_POOL_SCALE)).astype(bf16)
    mixer_out = jnp.concatenate(ys + [pool_o], axis=1)
    mix = []
    for j in range(D_MODEL // 256):
        yield 'x', 256 * k
        mix.append(_dot(mixer_out, w_out_ref[:, j * 256:(j + 1) * 256]))
    mix = jnp.concatenate(mix, axis=1)
    h1_parts, hn1_parts = [], []
    for rows in row_pieces:
        yield 'v', 300, LATE
        h1 = h_ref[rows] + _rms(mix[rows], va(R_POST_MIX))
        h1_parts.append(h1)
        hn1_parts.append(_rms(h1, va(R_PRE_FFN)).astype(bf16))
    out['h1'] = jnp.concatenate(h1_parts, axis=0)
    out['hn1'] = jnp.concatenate(hn1_parts, axis=0)


def _ple_finish(h_out_ref, h_ref, e_ref, gate_ref, va):
    for r in range(0, h_ref.shape[0], ROW_PIECE):
        yield 'v', 300
        rows = slice(r, r + ROW_PIECE)
        h_out_ref[rows] = h_ref[rows] + _rms(e_ref[rows] * _sigmoid(gate_ref[rows]), va(R_PLE))


def _ffn_ple_steps(h_out_ref, h1_ref, hn1_ref, p_ref, va, w_gate_ref, w_up_ref, w_down_ref, w_pg_ref,
                   w_pp_ref, scratch, *, deferred):
    act_scr, f_scr, gate_scr, h2_scr, e_scr = scratch
    k = max(1, h1_ref.shape[0] // 256)
    if deferred:
        yield from _ple_finish(h_out_ref, h2_scr, e_scr, gate_scr, va)
    nout = D_MODEL // 256
    for bi, (lo, hi) in enumerate(FF_BLOCKS):
        for c0 in range(lo, hi, 256):
            yield 'x', 256 * k
            gt = _dot(hn1_ref[...], w_gate_ref[:, c0:c0 + 256])
            yield 'x', 256 * k
            up = _dot(hn1_ref[...], w_up_ref[:, c0:c0 + 256])
            yield 'v', 130 * k
            act_scr[:, c0 - lo:c0 - lo + 256] = (gt * _sigmoid(gt) * up).astype(bf16)
        for j in range(nout):
            yield 'x', (hi - lo) // 4 * k
            cols = slice(j * 256, (j + 1) * 256)
            part = _dot(act_scr[:, 0:hi - lo], w_down_ref[lo:hi, cols])
            f_scr[:, cols] = part if bi == 0 else f_scr[:, cols] + part
    hb = []
    for r in range(0, h1_ref.shape[0], ROW_PIECE):
        yield 'v', 150
        rows = slice(r, r + ROW_PIECE)
        h = h1_ref[rows] + _rms(f_scr[rows], va(R_POST_FFN))
        h2_scr[rows] = h
        hb.append(h.astype(bf16))
    hb = jnp.concatenate(hb, axis=0)
    for j in range(nout):
        yield 'x', 256 * k, LATE
        gate_scr[:, j * 256:(j + 1) * 256] = _dot(hb, w_pg_ref[:, j * 256:(j + 1) * 256])
    yield 'x', 256 * k, LATE
    e_scr[...] = _dot(p_ref[...].astype(bf16), w_pp_ref[...])
    if not deferred:
        yield from _ple_finish(h_out_ref, h2_scr, e_scr, gate_scr, va)


def _merge(lead, fill=None):
    spent = {'x': 0.0, 'v': 0.0}

    def advance(stream, head):
        spent[head[0]] += head[1]
        return next(stream, None)

    lead_head = next(lead, None)
    fill_head = next(fill, None) if fill is not None else None
    while lead_head is not None:
        while (fill_head is not None and (LATE in lead_head or LATE not in fill_head)
               and spent['x'] <= MATRIX_TO_VECTOR_TIME * spent['v']):
            fill_head = advance(fill, fill_head)
        lead_head = advance(lead, lead_head)
    while fill_head is not None:
        fill_head = advance(fill, fill_head)


def _row_getters(va_ref, vb_ref):
    return (lambda r: va_ref[r:r + 1, :]), (lambda r: vb_ref[r:r + 1, :])


def _prompt_kernel(h_ref, h_next_ref, p_ref, va_ref, vb_ref, w_in_ref, pool_w_ref, w_out_ref, w_gate_ref,
                   w_up_ref, w_down_ref, w_pg_ref, w_pp_ref,
                   h_out_ref, ssm_out_ref, conv_out_ref, pool_out_ref,
                   xbc_ext, pool_ext, st_ref, y_scr, h1_scr, hn1_scr, act_scr, f_scr, gate_scr, h2_scr, e_scr,
                   hn_scr, *, tiles_per_seq, tile):
    n = pl.program_id(0)
    t = n % tiles_per_seq
    va, vb = _row_getters(va_ref, vb_ref)

    @pl.when(n == 0)
    def _():
        for ref in (h1_scr, hn1_scr, f_scr, gate_scr, h2_scr, e_scr):
            ref[...] = jnp.zeros(ref.shape, ref.dtype)
        hn_scr[...] = _rms(h_ref[...], va(R_PRE_MIX)).astype(bf16)

    @pl.when(t == 0)
    def _():
        xbc_ext[:, 0:HDR, :] = jnp.zeros((1, HDR, CONV_DIM), f32)
        pool_ext[:, 0:HDR, :] = jnp.zeros((1, HDR, D_POOL), f32)
        st_ref[...] = jnp.zeros(st_ref.shape, f32)

    out = {}
    mixer = _mixer_steps(out, h_ref, (None, None, None), va, vb, w_in_ref, pool_w_ref, w_out_ref,
                         (None, conv_out_ref, pool_out_ref), (xbc_ext, pool_ext, st_ref, y_scr),
                         nseg=1, seg_len=tile, t=t, prenorm=(hn_scr, h_next_ref))
    ffn = _ffn_ple_steps(h_out_ref, h1_scr, hn1_scr, p_ref, va, w_gate_ref, w_up_ref, w_down_ref, w_pg_ref,
                         w_pp_ref, (act_scr, f_scr, gate_scr, h2_scr, e_scr), deferred=True)
    _merge(mixer, ffn)
    h1_scr[...] = out['h1']
    hn1_scr[...] = out['hn1']

    @pl.when(t == tiles_per_seq - 1)
    def _():
        for g in range(SSM_GROUPS):
            ssm_out_ref[g * HPG:(g + 1) * HPG] = st_ref[g].T.reshape(HPG, SSM_HEAD_DIM, D_STATE)


def _sample_kernel(h_ref, p_ref, ssm0_ref, conv0_ref, pool0_ref, va_ref, vb_ref, w_in_ref, pool_w_ref,
                   w_out_ref, w_gate_ref, w_up_ref, w_down_ref, w_pg_ref, w_pp_ref,
                   h_out_ref, ssm_out_ref, conv_out_ref, pool_out_ref,
                   xbc_ext, pool_ext, st_ref, y_scr, h1_scr, hn1_scr, act_scr, f_scr, gate_scr, h2_scr, e_scr,
                   *, nseg, seg_len):
    va, vb = _row_getters(va_ref, vb_ref)
    out = {}
    _merge(_mixer_steps(out, h_ref, (ssm0_ref, conv0_ref, pool0_ref), va, vb, w_in_ref, pool_w_ref,
                        w_out_ref, (ssm_out_ref, conv_out_ref, pool_out_ref),
                        (xbc_ext, pool_ext, st_ref, y_scr), nseg=nseg, seg_len=seg_len, t=None))
    h1_scr[...] = out['h1']
    hn1_scr[...] = out['hn1']
    _merge(_ffn_ple_steps(h_out_ref, h1_scr, hn1_scr, p_ref, va, w_gate_ref, w_up_ref, w_down_ref, w_pg_ref,
                          w_pp_ref, (act_scr, f_scr, gate_scr, h2_scr, e_scr), deferred=False))


def _resident(shape_tail, layer):
    nd = len(shape_tail)
    return pl.BlockSpec((None,) + tuple(shape_tail), lambda *_: (layer,) + (0,) * nd,
                        pipeline_mode=pl.Buffered(1))


def _weight_specs(layer):
    return [
        _resident((16, D_MODEL), layer),
        _resident((8, D_SSM), layer),
        _resident((D_MODEL, IN_EXT), layer),
        _resident((2, 256, 256), layer),
        _resident((D_MODEL, D_MODEL), layer),
        _resident((D_MODEL, D_FF), layer),
        _resident((D_MODEL, D_FF), layer),
        _resident((D_FF, D_MODEL), layer),
        _resident((D_MODEL, D_MODEL), layer),
        _resident((D_PLE, D_MODEL), layer),
    ]


def _scratch(nseg, seg_len, pipelined):
    m = nseg * seg_len
    pre = [pltpu.VMEM((m, D_MODEL), bf16)] if pipelined else []
    return [
        pltpu.VMEM((nseg, HDR + seg_len, CONV_DIM), f32),
        pltpu.VMEM((nseg, HDR + seg_len, D_POOL), f32),
        pltpu.VMEM((SSM_GROUPS, D_STATE, GROUP_W), f32),
        pltpu.VMEM((m, D_SSM), f32),
        pltpu.VMEM((m, D_MODEL), f32),
        pltpu.VMEM((m, D_MODEL), bf16),
        pltpu.VMEM((m, max(hi - lo for lo, hi in FF_BLOCKS)), bf16),
        pltpu.VMEM((m, D_MODEL), f32),
        pltpu.VMEM((m, D_MODEL), f32),
        pltpu.VMEM((m, D_MODEL), f32),
        pltpu.VMEM((m, D_MODEL), f32),
    ] + pre


def _prompt_layer(layer, h, p_all, weights):
    nb, nl, _ = h.shape
    T = PROMPT_TILE
    assert nl % T == 0 and T % Q == 0
    nt = nl // T
    ntiles = nb * nt

    def tile_at(offset):
        def index(n):
            m = jnp.clip(n + offset, 0, ntiles - 1)
            return m // nt, m % nt
        return index

    nxt, cur, prev, prev2 = tile_at(1), tile_at(0), tile_at(-1), tile_at(-2)

    in_specs = [
        pl.BlockSpec((None, T, D_MODEL), lambda n: (*cur(n), 0)),
        pl.BlockSpec((None, T, D_MODEL), lambda n: (*nxt(n), 0)),
        pl.BlockSpec((None, None, T, D_PLE), lambda n: (layer, *prev(n), 0)),
    ] + _weight_specs(layer)
    out_shape = (
        jax.ShapeDtypeStruct((nb, nl, D_MODEL), f32),
        jax.ShapeDtypeStruct((nb, SSM_HEADS, SSM_HEAD_DIM, D_STATE), f32),
        jax.ShapeDtypeStruct((nb, 1, CONV_W - 1, CONV_DIM), f32),
        jax.ShapeDtypeStruct((nb, 1, POOL_HIST, D_POOL), f32),
    )
    out_specs = (
        pl.BlockSpec((None, T, D_MODEL), lambda n: (*prev2(n), 0)),
        pl.BlockSpec((None, SSM_HEADS, SSM_HEAD_DIM, D_STATE), lambda n: (cur(n)[0], 0, 0, 0)),
        pl.BlockSpec((None, 1, CONV_W - 1, CONV_DIM), lambda n: (cur(n)[0], 0, 0, 0)),
        pl.BlockSpec((None, 1, POOL_HIST, D_POOL), lambda n: (cur(n)[0], 0, 0, 0)),
    )
    h_new, ssm, conv, pool = pl.pallas_call(
        functools.partial(_prompt_kernel, tiles_per_seq=nt, tile=T),
        grid=(ntiles + 2,), in_specs=in_specs, out_specs=out_specs, out_shape=out_shape,
        scratch_shapes=_scratch(1, T, True),
        compiler_params=pltpu.CompilerParams(
            dimension_semantics=("arbitrary",), vmem_limit_bytes=VMEM_LIMIT),
        name=f"prompt_layer{layer}",
    )(h, h, p_all, *weights)
    return h_new, ssm, conv[:, 0], pool[:, 0]


def _sample_layer(layer, h2d, p_all, ssm_all, conv_all, pool_all, weights, nb, nl):
    assert nl % Q == 0
    M = nb * nl
    full = lambda *tail: pl.BlockSpec((None,) + tail, lambda i: (layer,) + (0,) * len(tail))
    in_specs = [
        pl.BlockSpec((M, D_MODEL), lambda i: (0, 0)),
        full(M, D_PLE),
        full(nb, SSM_HEADS, SSM_HEAD_DIM, D_STATE),
        full(nb, CONV_W - 1, CONV_DIM),
        full(nb, POOL_HIST, D_POOL),
    ] + _weight_specs(layer)
    out_shape = (
        jax.ShapeDtypeStruct((M, D_MODEL), f32),
        jax.ShapeDtypeStruct((nb, SSM_HEADS, SSM_HEAD_DIM, D_STATE), f32),
        jax.ShapeDtypeStruct((nb, CONV_W - 1, CONV_DIM), f32),
        jax.ShapeDtypeStruct((nb, POOL_HIST, D_POOL), f32),
    )
    out_specs = tuple(pl.BlockSpec(s.shape, lambda i, n=len(s.shape): (0,) * n) for s in out_shape)
    return pl.pallas_call(
        functools.partial(_sample_kernel, nseg=nb, seg_len=nl),
        grid=(1,), in_specs=in_specs, out_specs=out_specs, out_shape=out_shape,
        scratch_shapes=_scratch(nb, nl, False),
        compiler_params=pltpu.CompilerParams(
            dimension_semantics=("arbitrary",), vmem_limit_bytes=VMEM_LIMIT),
        name=f"sample_layer{layer}",
    )(h2d, p_all, ssm_all, conv_all, pool_all, *weights)


def _pack_weights(pre_mix_g, w_in, conv_w, conv_b, dt_bias, a_log, d_skip, ssm_norm_g, pool_w, pool_b,
                  pool_scale, w_out, post_mix_g, pre_ffn_g, w_gate, w_up, w_down, post_ffn_g, w_ple_gate,
                  w_ple_proj, ple_norm_g):
    dt_cols = D_SSM + CONV_DIM
    w_in_ext = jnp.concatenate([
        w_in[:, :, 0:dt_cols],
        w_in[:, :, dt_cols + SSM_HEADS:],
        jnp.repeat(w_in[:, :, dt_cols:dt_cols + SSM_HEADS], SSM_HEAD_DIM, axis=2),
    ], axis=2).astype(bf16)
    rep = lambda v: jnp.repeat(v, SSM_HEAD_DIM, axis=1)
    va = jnp.stack([pre_mix_g, post_mix_g, pre_ffn_g, post_ffn_g, ple_norm_g, conv_b]
                   + [conv_w[:, k] for k in range(CONV_W)], axis=1)
    va = jnp.pad(va, ((0, 0), (0, 16 - va.shape[1]), (0, 0)))
    vb = jnp.stack([rep(dt_bias), rep(a_log), rep(d_skip), ssm_norm_g,
                    pool_b.reshape(DEPTH, D_POOL), pool_scale], axis=1)
    vb = jnp.pad(vb, ((0, 0), (0, 8 - vb.shape[1]), (0, 0)))
    zeros = jnp.zeros((DEPTH, POOL_GROUP, POOL_GROUP), f32)
    pw = [jnp.concatenate([jnp.concatenate([pool_w[:, 2 * k], zeros], axis=2),
                           jnp.concatenate([zeros, pool_w[:, 2 * k + 1]], axis=2)], axis=1)
          for k in range(2)]
    pool_w2 = jnp.stack(pw, axis=1).astype(bf16)
    return (va, vb, w_in_ext, pool_w2, w_out.astype(bf16), w_gate.astype(bf16), w_up.astype(bf16),
            w_down.astype(bf16), w_ple_gate.astype(bf16), w_ple_proj.astype(bf16))


def kernel(x_prompt, x_sample, state_ssm, state_conv, state_pool, p_prompt, p_sample, pre_mix_g, w_in, conv_w, conv_b, dt_bias, a_log, d_skip, ssm_norm_g, pool_w, pool_b, pool_scale, w_out, post_mix_g, pre_ffn_g, w_gate, w_up, w_down, post_ffn_g, w_ple_gate, w_ple_proj, ple_norm_g):
    weights = _pack_weights(pre_mix_g, w_in, conv_w, conv_b, dt_bias, a_log, d_skip, ssm_norm_g, pool_w,
                            pool_b, pool_scale, w_out, post_mix_g, pre_ffn_g, w_gate, w_up, w_down,
                            post_ffn_g, w_ple_gate, w_ple_proj, ple_norm_g)
    nbs, nls, _ = x_sample.shape
    hp = x_prompt
    hs = x_sample.reshape(nbs * nls, D_MODEL)
    p_sample2d = p_sample.reshape(DEPTH, nbs * nls, D_PLE)
    outs_p, outs_s = [], []
    for i in range(DEPTH):
        hp, ssm, conv, pool = _prompt_layer(i, hp, p_prompt, weights)
        outs_p.append((ssm, conv, pool))
        hs, ssm, conv, pool = _sample_layer(i, hs, p_sample2d, state_ssm, state_conv, state_pool,
                                            weights, nbs, nls)
        outs_s.append((ssm, conv, pool))
    stack = lambda outs, j: jnp.stack([o[j] for o in outs])
    return (hp, hs.reshape(nbs, nls, D_MODEL),
            stack(outs_p, 0), stack(outs_p, 1), stack(outs_p, 2),
            stack(outs_s, 0), stack(outs_s, 1), stack(outs_s, 2))
```

```python
import functools

import jax
import jax.numpy as jnp
from jax import lax
from jax.experimental import pallas as pl
from jax.experimental.pallas import tpu as pltpu

D_MODEL = 1024
DEPTH = 4
PAST_LEN = 4096
D_SSM = 512
D_POOL = 512
SSM_HEAD_DIM = 64
SSM_HEADS = 8
SSM_GROUPS = 2
HPG = SSM_HEADS // SSM_GROUPS
D_STATE = 128
CONV_W = 4
CONV_DIM = 1024
POOL_WINDOWS = (2, 4, 8, 16)
POOL_GROUP = 128
POOL_HIST = 15
D_FF = 2816
D_PLE = 256
EPS = 1e-6

Q = 64
GROUP_W = D_SSM // SSM_GROUPS
HDR = 16
PROMPT_TILE = 256
FF_BLOCKS = ((0, 1024), (1024, 2048), (2048, D_FF))
NEG = -1e30
LATE = 'late'
ROW_PIECE = 128
MATRIX_TO_VECTOR_TIME = 1.6
VMEM_CAPACITY_V7X = 64 * 1024 * 1024
VMEM_LIMIT = VMEM_CAPACITY_V7X - 8 * 1024 * 1024

Z0, XBC0, U0, DT0, IN_EXT = 0, 512, 1536, 2048, 2560

(R_PRE_MIX, R_POST_MIX, R_PRE_FFN, R_POST_FFN, R_PLE, R_CONV_B, R_CONV_W0) = range(7)
(R_DT_BIAS, R_A_LOG, R_D_SKIP, R_SSM_G, R_POOL_B, R_POOL_SCALE) = range(6)

f32 = jnp.float32
bf16 = jnp.bfloat16


def _rms(x, g_row):
    ms = jnp.mean(x * x, axis=-1, keepdims=True)
    return x * lax.rsqrt(ms + EPS) * g_row


def _sigmoid(x):
    return 1.0 / (1.0 + jnp.exp(-x))


def _dot(a, b):
    return jnp.dot(a, b, preferred_element_type=f32)


def _done_token(x):
    bits = pltpu.bitcast(x, jnp.uint32)
    acc = bits[:, 0:128]
    for l in range(128, x.shape[1], 128):
        acc = acc | bits[:, l:l + 128]
    rows = acc[0:8]
    for r in range(8, x.shape[0], 8):
        rows = rows | acc[r:r + 8]
    zero = lax.shift_right_logical(lax.shift_right_logical(rows, jnp.uint32(16)), jnp.uint32(16))
    return zero[0:1].astype(f32)


def _take_tokens(tokens, row):
    while tokens:
        row = row + jnp.concatenate([tokens.pop()] * (row.shape[1] // 128), axis=1)
    return row


def _chunk_cumsum(x):
    row = lax.broadcasted_iota(jnp.int32, x.shape, 0) % Q
    sh = 1
    while sh < Q:
        x = x + jnp.where(row >= sh, pltpu.roll(x, sh, axis=0), 0.0)
        sh *= 2
    return x


def _ssd_block(xg, dtg, ag, bg, cg, s, masks):
    eye2, causal2, blockmask = masks
    a_last = ag[Q - 1:Q, :]
    xdt = xg * dtg
    bb = bg.astype(bf16)
    cb_ = cg.astype(bf16)
    b2 = jnp.concatenate([bb, bb], axis=0)
    cb2 = lax.dot_general(cb_, b2, (((1,), (1,)), ((), ())),
                          preferred_element_type=f32)
    ys = []
    for q in range(2):
        p_col = ag[:, q * 128:(q + 1) * 128]
        p_row = jnp.sum(jnp.where(eye2, p_col, 0.0), axis=0, keepdims=True)
        lmat = jnp.exp(jnp.where(causal2, p_col - p_row, NEG))
        g = (cb2 * lmat).astype(bf16)
        xp = xdt[:, q * 128:(q + 1) * 128]
        bd = jnp.where(blockmask, jnp.concatenate([xp, xp], axis=0), 0.0).astype(bf16)
        ys.append(_dot(g, bd))
    y_diag = jnp.concatenate(ys, axis=1)
    y_off = _dot(cb_, s.astype(bf16)) * jnp.exp(ag)
    xdtd = (xdt * jnp.exp(a_last - ag)).astype(bf16)
    states = _dot(bg.T.astype(bf16), xdtd)
    s_new = s * jnp.exp(a_last) + states
    return y_diag + y_off, s_new


def _mixer_steps(out, h_ref, state_in, va, vb, w_in_ref, pool_w_ref, w_out_ref, state_out, scratch,
                 *, nseg, seg_len, t, prenorm=None):
    streaming = t is not None
    ssm0_ref, conv0_ref, pool0_ref = state_in
    ssm_out_ref, conv_out_ref, pool_out_ref = state_out
    xbc_ext, pool_ext, st_ref, y_scr = scratch
    L = seg_len
    k = max(1, nseg * L // 256)
    W = GROUP_W
    seg_rows = [slice(s * L, (s + 1) * L) for s in range(nseg)]
    row_pieces = [slice(r, r + ROW_PIECE) for r in range(0, nseg * L, ROW_PIECE)]
    H = W // 2
    prenorm_done = []

    if prenorm is None:
        yield 'v', 320 * k
        hn_val = _rms(h_ref[...], va(R_PRE_MIX)).astype(bf16)
        hn = lambda: hn_val
    else:
        hn = lambda: prenorm[0][...]
    if not streaming:
        for s in range(nseg):
            xbc_ext[s, HDR - (CONV_W - 1):HDR, :] = conv0_ref[s]
            pool_ext[s, 0:HDR - POOL_HIST, :] = jnp.zeros((HDR - POOL_HIST, D_POOL), f32)
            pool_ext[s, HDR - POOL_HIST:HDR, :] = pool0_ref[s]

    dts = []
    for g in range(SSM_GROUPS):
        yield 'x', 256 * k
        dts.append(_dot(hn(), w_in_ref[:, DT0 + g * W:DT0 + (g + 1) * W]))
    for j in range(CONV_DIM // W):
        yield 'x', 256 * k
        piece = _dot(hn(), w_in_ref[:, XBC0 + j * W:XBC0 + (j + 1) * W])
        for s in range(nseg):
            xbc_ext[s, HDR:HDR + L, j * W:(j + 1) * W] = piece[seg_rows[s]]
    for s in range(nseg):
        conv_out_ref[s] = xbc_ext[s, HDR + L - (CONV_W - 1):HDR + L, :]

    acums = []
    for g in range(SSM_GROUPS):
        dt_halves, acum_halves = [], []
        for l0 in range(g * W, (g + 1) * W, H):
            yield 'v', 170 * k
            bias = _take_tokens(out.setdefault('finish_done', []), vb(R_DT_BIAS)[:, l0:l0 + H])
            d = dts[g][:, l0 - g * W:l0 - g * W + H] + bias
            d = jnp.maximum(d, 0.0) + jnp.log(1.0 + jnp.exp(-jnp.abs(d)))
            dt_halves.append(d)
            acum_halves.append(_chunk_cumsum(d * -jnp.exp(vb(R_A_LOG)[:, l0:l0 + H])))
        dts[g] = jnp.concatenate(dt_halves, axis=1)
        acums.append(jnp.concatenate(acum_halves, axis=1))

    xbc = []
    for j in range(CONV_DIM // H):
        yield 'v', 200 * k
        ls = slice(j * H, (j + 1) * H)
        parts = []
        for s in range(nseg):
            acc = va(R_CONV_B)[:, ls]
            for kk in range(CONV_W):
                lo = HDR - (CONV_W - 1) + kk
                acc = acc + va(R_CONV_W0 + kk)[:, ls] * xbc_ext[s, lo:lo + L, ls]
            parts.append(acc)
        piece = parts[0] if nseg == 1 else jnp.concatenate(parts, axis=0)
        xbc.append(piece * _sigmoid(piece))
    xbc = [jnp.concatenate(xbc[j:j + W // H], axis=1) for j in range(0, len(xbc), W // H)]
    xs, bmat, cmat = xbc[0:SSM_GROUPS], xbc[SSM_GROUPS], xbc[SSM_GROUPS + 1]
    if streaming:
        xbc_ext[:, 0:HDR, :] = xbc_ext[:, L:L + HDR, :]

    zs = []
    for g in range(SSM_GROUPS):
        yield 'x', 256 * k
        zs.append(_dot(hn(), w_in_ref[:, Z0 + g * W:Z0 + (g + 1) * W]))
    for j in range(D_POOL // W):
        yield 'x', 256 * k
        piece = _dot(hn(), w_in_ref[:, U0 + j * W:U0 + (j + 1) * W])
        for s in range(nseg):
            pool_ext[s, HDR:HDR + L, j * W:(j + 1) * W] = piece[seg_rows[s]]
    for s in range(nseg):
        pool_out_ref[s] = pool_ext[s, L + HDR - POOL_HIST:L + HDR, :]
    if prenorm is not None:
        for rows in row_pieces:
            yield 'v', 160
            hn_next = _rms(prenorm[1][rows], va(R_PRE_MIX))
            prenorm[0][rows] = hn_next.astype(bf16)
            prenorm_done.append(_done_token(hn_next))

    lane = lax.broadcasted_iota(jnp.int32, (Q, 128), 1)
    rowq = lax.broadcasted_iota(jnp.int32, (Q, 128), 0)
    eye2 = rowq == (lane % Q)
    causal2 = rowq >= (lane % Q)
    row2 = lax.broadcasted_iota(jnp.int32, (2 * Q, 128), 0)
    lane2 = lax.broadcasted_iota(jnp.int32, (2 * Q, 128), 1)
    blockmask = (row2 // Q) == (lane2 // SSM_HEAD_DIM)
    masks = (eye2, causal2, blockmask)

    for s in range(nseg):
        states = []
        for g in range(SSM_GROUPS):
            if streaming:
                states.append(st_ref[g])
            else:
                states.append(ssm0_ref[s, g * HPG:(g + 1) * HPG].reshape(W, D_STATE).T)
        for c in range(L // Q):
            rs = slice(s * L + c * Q, s * L + (c + 1) * Q)
            for g in range(SSM_GROUPS):
                yield 'v', 200
                bg = bmat[rs, g * D_STATE:(g + 1) * D_STATE]
                cg = cmat[rs, g * D_STATE:(g + 1) * D_STATE]
                yg, states[g] = _ssd_block(xs[g][rs], dts[g][rs], acums[g][rs], bg, cg, states[g], masks)
                y_scr[rs, g * W:(g + 1) * W] = yg
        for g in range(SSM_GROUPS):
            if streaming:
                st_ref[g] = states[g]
            else:
                ssm_out_ref[s, g * HPG:(g + 1) * HPG] = states[g].T.reshape(HPG, SSM_HEAD_DIM, D_STATE)

    ys = []
    for g in range(SSM_GROUPS):
        gl = slice(g * W, (g + 1) * W)
        parts = []
        for rows in row_pieces:
            yield 'v', 130
            d_skip = _take_tokens(prenorm_done, vb(R_D_SKIP)[:, gl])
            yg = y_scr[rows, gl] + d_skip * xs[g][rows]
            yg = yg * (zs[g][rows] * _sigmoid(zs[g][rows]))
            yg = yg * lax.rsqrt(jnp.mean(yg * yg, axis=-1, keepdims=True) + EPS)
            parts.append((yg * vb(R_SSM_G)[:, gl]).astype(bf16))
        ys.append(jnp.concatenate(parts, axis=0))

    pos0 = t * L if streaming else PAST_LEN
    pos1 = pos0 + 1 + lax.broadcasted_iota(jnp.int32, (L, POOL_GROUP), 0)
    assert POOL_WINDOWS == tuple(2 ** (i + 1) for i in range(len(POOL_WINDOWS))) and HDR > POOL_HIST
    pooled = []
    sums = None
    for gi, w in enumerate(POOL_WINDOWS):
        yield 'v', (150 - 30 * gi) * k
        if sums is None:
            sums = [pool_ext[s] for s in range(nseg)]
        sums = [x + pltpu.roll(x, w // 2, axis=0) for x in sums]
        cnt = jnp.minimum(pos1, w).astype(f32)
        ls = slice(gi * POOL_GROUP, (gi + 1) * POOL_GROUP)
        parts = [x[HDR:HDR + L, 0:POOL_GROUP] / cnt - pool_ext[s, HDR:HDR + L, ls]
                 for s, x in enumerate(sums)]
        piece = parts[0] if nseg == 1 else jnp.concatenate(parts, axis=0)
        pooled.append(piece.astype(bf16))
        sums = [x[:, POOL_GROUP:] for x in sums]
    if streaming:
        pool_ext[:, 0:HDR, :] = pool_ext[:, L:L + HDR, :]

    yield 'x', 128 * k
    pool_o = jnp.concatenate([_dot(jnp.concatenate(pooled[0:2], axis=1), pool_w_ref[0]),
                              _dot(jnp.concatenate(pooled[2:4], axis=1), pool_w_ref[1])], axis=1)
    pool_o = ((pool_o + vb(R_POOL_B)) * vb(R_POOL_SCALE)).astype(bf16)
    mixer_out = jnp.concatenate(ys + [pool_o], axis=1)
    mix = []
    for j in range(D_MODEL // 256):
        yield 'x', 256 * k
        mix.append(_dot(mixer_out, w_out_ref[:, j * 256:(j + 1) * 256]))
    mix = jnp.concatenate(mix, axis=1)
    h1_parts, hn1_parts = [], []
    for rows in row_pieces:
        yield 'v', 300, LATE
        h1 = h_ref[rows] + _rms(mix[rows], va(R_POST_MIX))
        h1_parts.append(h1)
        hn1_parts.append(_rms(h1, va(R_PRE_FFN)).astype(bf16))
    out['h1'] = jnp.concatenate(h1_parts, axis=0)
    out['hn1'] = jnp.concatenate(hn1_parts, axis=0)


def _ple_finish(h_out_ref, h_ref, e_ref, gate_ref, va, done=None):
    for r in range(0, h_ref.shape[0], ROW_PIECE):
        yield 'v', 300
        rows = slice(r, r + ROW_PIECE)
        res = h_ref[rows] + _rms(e_ref[rows] * _sigmoid(gate_ref[rows]), va(R_PLE))
        h_out_ref[rows] = res
        if done is not None:
            done.append(_done_token(res))


def _ffn_ple_steps(h_out_ref, h1_ref, hn1_ref, p_ref, va, w_gate_ref, w_up_ref, w_down_ref, w_pg_ref,
                   w_pp_ref, scratch, *, deferred, finish_done=None):
    act_scr, f_scr, gate_scr, h2_scr, e_scr = scratch
    k = max(1, h1_ref.shape[0] // 256)
    if deferred:
        yield from _ple_finish(h_out_ref, h2_scr, e_scr, gate_scr, va, finish_done)
    nout = D_MODEL // 256
    for bi, (lo, hi) in enumerate(FF_BLOCKS):
        for c0 in range(lo, hi, 256):
            yield 'x', 256 * k
            gt = _dot(hn1_ref[...], w_gate_ref[:, c0:c0 + 256])
            yield 'x', 256 * k
            up = _dot(hn1_ref[...], w_up_ref[:, c0:c0 + 256])
            yield 'v', 130 * k
            act_scr[:, c0 - lo:c0 - lo + 256] = (gt * _sigmoid(gt) * up).astype(bf16)
        for j in range(nout):
            yield 'x', (hi - lo) // 4 * k
            cols = slice(j * 256, (j + 1) * 256)
            part = _dot(act_scr[:, 0:hi - lo], w_down_ref[lo:hi, cols])
            f_scr[:, cols] = part if bi == 0 else f_scr[:, cols] + part
    hb = []
    for r in range(0, h1_ref.shape[0], ROW_PIECE):
        yield 'v', 150
        rows = slice(r, r + ROW_PIECE)
        h = h1_ref[rows] + _rms(f_scr[rows], va(R_POST_FFN))
        h2_scr[rows] = h
        hb.append(h.astype(bf16))
    hb = jnp.concatenate(hb, axis=0)
    for j in range(nout):
        yield 'x', 256 * k, LATE
        gate_scr[:, j * 256:(j + 1) * 256] = _dot(hb, w_pg_ref[:, j * 256:(j + 1) * 256])
    yield 'x', 256 * k, LATE
    e_scr[...] = _dot(p_ref[...].astype(bf16), w_pp_ref[...])
    if not deferred:
        yield from _ple_finish(h_out_ref, h2_scr, e_scr, gate_scr, va)


def _merge(lead, fill=None):
    spent = {'x': 0.0, 'v': 0.0}

    def advance(stream, head):
        spent[head[0]] += head[1]
        return next(stream, None)

    lead_head = next(lead, None)
    fill_head = next(fill, None) if fill is not None else None
    while lead_head is not None:
        while (fill_head is not None and (LATE in lead_head or LATE not in fill_head)
               and spent['x'] <= MATRIX_TO_VECTOR_TIME * spent['v']):
            fill_head = advance(fill, fill_head)
        lead_head = advance(lead, lead_head)
    while fill_head is not None:
        fill_head = advance(fill, fill_head)


def _row_getters(va_ref, vb_ref):
    return (lambda r: va_ref[r:r + 1, :]), (lambda r: vb_ref[r:r + 1, :])


def _prompt_kernel(h_ref, h_next_ref, p_ref, va_ref, vb_ref, w_in_ref, pool_w_ref, w_out_ref, w_gate_ref,
                   w_up_ref, w_down_ref, w_pg_ref, w_pp_ref,
                   h_out_ref, ssm_out_ref, conv_out_ref, pool_out_ref,
                   xbc_ext, pool_ext, st_ref, y_scr, h1_scr, hn1_scr, act_scr, f_scr, gate_scr, h2_scr, e_scr,
                   hn_scr, *, tiles_per_seq, tile):
    n = pl.program_id(0)
    t = n % tiles_per_seq
    va, vb = _row_getters(va_ref, vb_ref)

    @pl.when(n == 0)
    def _():
        for ref in (h1_scr, hn1_scr, f_scr, gate_scr, h2_scr, e_scr):
            ref[...] = jnp.zeros(ref.shape, ref.dtype)
        hn_scr[...] = _rms(h_ref[...], va(R_PRE_MIX)).astype(bf16)

    @pl.when(t == 0)
    def _():
        xbc_ext[:, 0:HDR, :] = jnp.zeros((1, HDR, CONV_DIM), f32)
        pool_ext[:, 0:HDR, :] = jnp.zeros((1, HDR, D_POOL), f32)
        st_ref[...] = jnp.zeros(st_ref.shape, f32)

    out = {'finish_done': []}
    mixer = _mixer_steps(out, h_ref, (None, None, None), va, vb, w_in_ref, pool_w_ref, w_out_ref,
                         (None, conv_out_ref, pool_out_ref), (xbc_ext, pool_ext, st_ref, y_scr),
                         nseg=1, seg_len=tile, t=t, prenorm=(hn_scr, h_next_ref))
    ffn = _ffn_ple_steps(h_out_ref, h1_scr, hn1_scr, p_ref, va, w_gate_ref, w_up_ref, w_down_ref, w_pg_ref,
                         w_pp_ref, (act_scr, f_scr, gate_scr, h2_scr, e_scr), deferred=True,
                         finish_done=out['finish_done'])
    _merge(mixer, ffn)
    h1_scr[...] = out['h1']
    hn1_scr[...] = out['hn1']

    @pl.when(t == tiles_per_seq - 1)
    def _():
        for g in range(SSM_GROUPS):
            ssm_out_ref[g * HPG:(g + 1) * HPG] = st_ref[g].T.reshape(HPG, SSM_HEAD_DIM, D_STATE)


def _sample_kernel(h_ref, p_ref, ssm0_ref, conv0_ref, pool0_ref, va_ref, vb_ref, w_in_ref, pool_w_ref,
                   w_out_ref, w_gate_ref, w_up_ref, w_down_ref, w_pg_ref, w_pp_ref,
                   h_out_ref, ssm_out_ref, conv_out_ref, pool_out_ref,
                   xbc_ext, pool_ext, st_ref, y_scr, h1_scr, hn1_scr, act_scr, f_scr, gate_scr, h2_scr, e_scr,
                   *, nseg, seg_len):
    va, vb = _row_getters(va_ref, vb_ref)
    out = {}
    _merge(_mixer_steps(out, h_ref, (ssm0_ref, conv0_ref, pool0_ref), va, vb, w_in_ref, pool_w_ref,
                        w_out_ref, (ssm_out_ref, conv_out_ref, pool_out_ref),
                        (xbc_ext, pool_ext, st_ref, y_scr), nseg=nseg, seg_len=seg_len, t=None))
    h1_scr[...] = out['h1']
    hn1_scr[...] = out['hn1']
    _merge(_ffn_ple_steps(h_out_ref, h1_scr, hn1_scr, p_ref, va, w_gate_ref, w_up_ref, w_down_ref, w_pg_ref,
                          w_pp_ref, (act_scr, f_scr, gate_scr, h2_scr, e_scr), deferred=False))


def _resident(shape_tail, layer):
    nd = len(shape_tail)
    return pl.BlockSpec((None,) + tuple(shape_tail), lambda *_: (layer,) + (0,) * nd,
                        pipeline_mode=pl.Buffered(1))


def _weight_specs(layer):
    return [
        _resident((16, D_MODEL), layer),
        _resident((8, D_SSM), layer),
        _resident((D_MODEL, IN_EXT), layer),
        _resident((2, 256, 256), layer),
        _resident((D_MODEL, D_MODEL), layer),
        _resident((D_MODEL, D_FF), layer),
        _resident((D_MODEL, D_FF), layer),
        _resident((D_FF, D_MODEL), layer),
        _resident((D_MODEL, D_MODEL), layer),
        _resident((D_PLE, D_MODEL), layer),
    ]


def _scratch(nseg, seg_len, pipelined):
    m = nseg * seg_len
    pre = [pltpu.VMEM((m, D_MODEL), bf16)] if pipelined else []
    return [
        pltpu.VMEM((nseg, HDR + seg_len, CONV_DIM), f32),
        pltpu.VMEM((nseg, HDR + seg_len, D_POOL), f32),
        pltpu.VMEM((SSM_GROUPS, D_STATE, GROUP_W), f32),
        pltpu.VMEM((m, D_SSM), f32),
        pltpu.VMEM((m, D_MODEL), f32),
        pltpu.VMEM((m, D_MODEL), bf16),
        pltpu.VMEM((m, max(hi - lo for lo, hi in FF_BLOCKS)), bf16),
        pltpu.VMEM((m, D_MODEL), f32),
        pltpu.VMEM((m, D_MODEL), f32),
        pltpu.VMEM((m, D_MODEL), f32),
        pltpu.VMEM((m, D_MODEL), f32),
    ] + pre


def _prompt_layer(layer, h, p_all, weights):
    nb, nl, _ = h.shape
    T = PROMPT_TILE
    assert nl % T == 0 and T % Q == 0
    nt = nl // T
    ntiles = nb * nt

    def tile_at(offset):
        def index(n):
            m = jnp.clip(n + offset, 0, ntiles - 1)
            return m // nt, m % nt
        return index

    nxt, cur, prev, prev2 = tile_at(1), tile_at(0), tile_at(-1), tile_at(-2)

    in_specs = [
        pl.BlockSpec((None, T, D_MODEL), lambda n: (*cur(n), 0)),
        pl.BlockSpec((None, T, D_MODEL), lambda n: (*nxt(n), 0)),
        pl.BlockSpec((None, None, T, D_PLE), lambda n: (layer, *prev(n), 0)),
    ] + _weight_specs(layer)
    out_shape = (
        jax.ShapeDtypeStruct((nb, nl, D_MODEL), f32),
        jax.ShapeDtypeStruct((nb, SSM_HEADS, SSM_HEAD_DIM, D_STATE), f32),
        jax.ShapeDtypeStruct((nb, 1, CONV_W - 1, CONV_DIM), f32),
        jax.ShapeDtypeStruct((nb, 1, POOL_HIST, D_POOL), f32),
    )
    out_specs = (
        pl.BlockSpec((None, T, D_MODEL), lambda n: (*prev2(n), 0)),
        pl.BlockSpec((None, SSM_HEADS, SSM_HEAD_DIM, D_STATE), lambda n: (cur(n)[0], 0, 0, 0)),
        pl.BlockSpec((None, 1, CONV_W - 1, CONV_DIM), lambda n: (cur(n)[0], 0, 0, 0)),
        pl.BlockSpec((None, 1, POOL_HIST, D_POOL), lambda n: (cur(n)[0], 0, 0, 0)),
    )
    h_new, ssm, conv, pool = pl.pallas_call(
        functools.partial(_prompt_kernel, tiles_per_seq=nt, tile=T),
        grid=(ntiles + 2,), in_specs=in_specs, out_specs=out_specs, out_shape=out_shape,
        scratch_shapes=_scratch(1, T, True),
        compiler_params=pltpu.CompilerParams(
            dimension_semantics=("arbitrary",), vmem_limit_bytes=VMEM_LIMIT),
        name=f"prompt_layer{layer}",
    )(h, h, p_all, *weights)
    return h_new, ssm, conv[:, 0], pool[:, 0]


def _sample_layer(layer, h2d, p_all, ssm_all, conv_all, pool_all, weights, nb, nl):
    assert nl % Q == 0
    M = nb * nl
    full = lambda *tail: pl.BlockSpec((None,) + tail, lambda i: (layer,) + (0,) * len(tail))
    in_specs = [
        pl.BlockSpec((M, D_MODEL), lambda i: (0, 0)),
        full(M, D_PLE),
        full(nb, SSM_HEADS, SSM_HEAD_DIM, D_STATE),
        full(nb, CONV_W - 1, CONV_DIM),
        full(nb, POOL_HIST, D_POOL),
    ] + _weight_specs(layer)
    out_shape = (
        jax.ShapeDtypeStruct((M, D_MODEL), f32),
        jax.ShapeDtypeStruct((nb, SSM_HEADS, SSM_HEAD_DIM, D_STATE), f32),
        jax.ShapeDtypeStruct((nb, CONV_W - 1, CONV_DIM), f32),
        jax.ShapeDtypeStruct((nb, POOL_HIST, D_POOL), f32),
    )
    out_specs = tuple(pl.BlockSpec(s.shape, lambda i, n=len(s.shape): (0,) * n) for s in out_shape)
    return pl.pallas_call(
        functools.partial(_sample_kernel, nseg=nb, seg_len=nl),
        grid=(1,), in_specs=in_specs, out_specs=out_specs, out_shape=out_shape,
        scratch_shapes=_scratch(nb, nl, False),
        compiler_params=pltpu.CompilerParams(
            dimension_semantics=("arbitrary",), vmem_limit_bytes=VMEM_LIMIT),
        name=f"sample_layer{layer}",
    )(h2d, p_all, ssm_all, conv_all, pool_all, *weights)


def _pack_weights(pre_mix_g, w_in, conv_w, conv_b, dt_bias, a_log, d_skip, ssm_norm_g, pool_w, pool_b,
                  pool_scale, w_out, post_mix_g, pre_ffn_g, w_gate, w_up, w_down, post_ffn_g, w_ple_gate,
                  w_ple_proj, ple_norm_g):
    dt_cols = D_SSM + CONV_DIM
    w_in_ext = jnp.concatenate([
        w_in[:, :, 0:dt_cols],
        w_in[:, :, dt_cols + SSM_HEADS:],
        jnp.repeat(w_in[:, :, dt_cols:dt_cols + SSM_HEADS], SSM_HEAD_DIM, axis=2),
    ], axis=2).astype(bf16)
    rep = lambda v: jnp.repeat(v, SSM_HEAD_DIM, axis=1)
    va = jnp.stack([pre_mix_g, post_mix_g, pre_ffn_g, post_ffn_g, ple_norm_g, conv_b]
                   + [conv_w[:, k] for k in range(CONV_W)], axis=1)
    va = jnp.pad(va, ((0, 0), (0, 16 - va.shape[1]), (0, 0)))
    vb = jnp.stack([rep(dt_bias), rep(a_log), rep(d_skip), ssm_norm_g,
                    pool_b.reshape(DEPTH, D_POOL), pool_scale], axis=1)
    vb = jnp.pad(vb, ((0, 0), (0, 8 - vb.shape[1]), (0, 0)))
    zeros = jnp.zeros((DEPTH, POOL_GROUP, POOL_GROUP), f32)
    pw = [jnp.concatenate([jnp.concatenate([pool_w[:, 2 * k], zeros], axis=2),
                           jnp.concatenate([zeros, pool_w[:, 2 * k + 1]], axis=2)], axis=1)
          for k in range(2)]
    pool_w2 = jnp.stack(pw, axis=1).astype(bf16)
    return (va, vb, w_in_ext, pool_w2, w_out.astype(bf16), w_gate.astype(bf16), w_up.astype(bf16),
            w_down.astype(bf16), w_ple_gate.astype(bf16), w_ple_proj.astype(bf16))


def kernel(x_prompt, x_sample, state_ssm, state_conv, state_pool, p_prompt, p_sample, pre_mix_g, w_in, conv_w, conv_b, dt_bias, a_log, d_skip, ssm_norm_g, pool_w, pool_b, pool_scale, w_out, post_mix_g, pre_ffn_g, w_gate, w_up, w_down, post_ffn_g, w_ple_gate, w_ple_proj, ple_norm_g):
    weights = _pack_weights(pre_mix_g, w_in, conv_w, conv_b, dt_bias, a_log, d_skip, ssm_norm_g, pool_w,
                            pool_b, pool_scale, w_out, post_mix_g, pre_ffn_g, w_gate, w_up, w_down,
                            post_ffn_g, w_ple_gate, w_ple_proj, ple_norm_g)
    nbs, nls, _ = x_sample.shape
    hp = x_prompt
    hs = x_sample.reshape(nbs * nls, D_MODEL)
    p_sample2d = p_sample.reshape(DEPTH, nbs * nls, D_PLE)
    outs_p, outs_s = [], []
    for i in range(DEPTH):
        hp, ssm, conv, pool = _prompt_layer(i, hp, p_prompt, weights)
        outs_p.append((ssm, conv, pool))
        hs, ssm, conv, pool = _sample_layer(i, hs, p_sample2d, state_ssm, state_conv, state_pool,
                                            weights, nbs, nls)
        outs_s.append((ssm, conv, pool))
    stack = lambda outs, j: jnp.stack([o[j] for o in outs])
    return (hp, hs.reshape(nbs, nls, D_MODEL),
            stack(outs_p, 0), stack(outs_p, 1), stack(outs_p, 2),
            stack(outs_s, 0), stack(outs_s, 1), stack(outs_s, 2))
```

```python
import functools

import jax
import jax.numpy as jnp
from jax import lax
from jax.experimental import pallas as pl
from jax.experimental.pallas import tpu as pltpu

D_MODEL = 1024
DEPTH = 4
PAST_LEN = 4096
D_SSM = 512
D_POOL = 512
SSM_HEAD_DIM = 64
SSM_HEADS = 8
SSM_GROUPS = 2
HPG = SSM_HEADS // SSM_GROUPS
D_STATE = 128
CONV_W = 4
CONV_DIM = 1024
POOL_WINDOWS = (2, 4, 8, 16)
POOL_GROUP = 128
POOL_HIST = 15
D_FF = 2816
D_PLE = 256
EPS = 1e-6

Q = 64
GROUP_W = D_SSM // SSM_GROUPS
HDR = 16
PROMPT_TILE = 256
FF_BLOCKS = ((0, 1024), (1024, 2048), (2048, D_FF))
NEG = -1e30
LANES, SUBLANES = 128, 8
MXU_TILE = 256
COST_ROWS = 256
X_PIECE = 256
LATE = 'late'
ROW_PIECE = 128
MATRIX_TO_VECTOR_TIME = 1.6
VMEM_CAPACITY_V7X = 64 * 1024 * 1024
VMEM_LIMIT = VMEM_CAPACITY_V7X - 8 * 1024 * 1024

Z0, XBC0, U0, DT0, IN_EXT = 0, 512, 1536, 2048, 2560

(R_PRE_MIX, R_POST_MIX, R_PRE_FFN, R_POST_FFN, R_PLE, R_CONV_B, R_CONV_W0) = range(7)
(R_DT_BIAS, R_A_LOG, R_D_SKIP, R_SSM_G, R_POOL_B, R_POOL_SCALE) = range(6)

f32 = jnp.float32
bf16 = jnp.bfloat16


def _rms(x, g_row):
    ms = jnp.mean(x * x, axis=-1, keepdims=True)
    return x * lax.rsqrt(ms + EPS) * g_row


def _sigmoid(x):
    return 1.0 / (1.0 + jnp.exp(-x))


def _dot(a, b):
    return jnp.dot(a, b, preferred_element_type=f32)


def _done_token(x):
    bits = pltpu.bitcast(x, jnp.uint32)
    acc = bits[:, 0:LANES]
    for l in range(LANES, x.shape[1], LANES):
        acc = acc | bits[:, l:l + LANES]
    rows = acc[0:SUBLANES]
    for r in range(SUBLANES, x.shape[0], SUBLANES):
        rows = rows | acc[r:r + SUBLANES]
    zero = lax.shift_right_logical(lax.shift_right_logical(rows, jnp.uint32(16)), jnp.uint32(16))
    return zero[0:1].astype(f32)


def _take_tokens(tokens, row):
    while tokens:
        row = row + jnp.concatenate([tokens.pop()] * (row.shape[1] // LANES), axis=1)
    return row


def _chunk_cumsum(x):
    row = lax.broadcasted_iota(jnp.int32, x.shape, 0) % Q
    sh = 1
    while sh < Q:
        x = x + jnp.where(row >= sh, pltpu.roll(x, sh, axis=0), 0.0)
        sh *= 2
    return x


def _ssd_block(xg, dtg, ag, bg, cg, s, masks):
    eye2, causal2, blockmask = masks
    a_last = ag[Q - 1:Q, :]
    xdt = xg * dtg
    bb = bg.astype(bf16)
    cb_ = cg.astype(bf16)
    b2 = jnp.concatenate([bb, bb], axis=0)
    cb2 = lax.dot_general(cb_, b2, (((1,), (1,)), ((), ())),
                          preferred_element_type=f32)
    ys = []
    for q in range(2):
        p_col = ag[:, q * LANES:(q + 1) * LANES]
        p_row = jnp.sum(jnp.where(eye2, p_col, 0.0), axis=0, keepdims=True)
        lmat = jnp.exp(jnp.where(causal2, p_col - p_row, NEG))
        g = (cb2 * lmat).astype(bf16)
        xp = xdt[:, q * LANES:(q + 1) * LANES]
        bd = jnp.where(blockmask, jnp.concatenate([xp, xp], axis=0), 0.0).astype(bf16)
        ys.append(_dot(g, bd))
    y_diag = jnp.concatenate(ys, axis=1)
    y_off = _dot(cb_, s.astype(bf16)) * jnp.exp(ag)
    xdtd = (xdt * jnp.exp(a_last - ag)).astype(bf16)
    states = _dot(bg.T.astype(bf16), xdtd)
    s_new = s * jnp.exp(a_last) + states
    return y_diag + y_off, s_new


def _mixer_steps(out, h_ref, state_in, va, vb, w_in_ref, pool_w_ref, w_out_ref, state_out, scratch,
                 *, nseg, seg_len, t, prenorm=None):
    streaming = t is not None
    ssm0_ref, conv0_ref, pool0_ref = state_in
    ssm_out_ref, conv_out_ref, pool_out_ref = state_out
    xbc_ext, pool_ext, st_ref, y_scr = scratch
    L = seg_len
    k = max(1, nseg * L // COST_ROWS)
    W = GROUP_W
    seg_rows = [slice(s * L, (s + 1) * L) for s in range(nseg)]
    row_pieces = [slice(r, r + ROW_PIECE) for r in range(0, nseg * L, ROW_PIECE)]
    H = W // 2
    prenorm_done = []

    if prenorm is None:
        yield 'v', 320 * k
        hn_val = _rms(h_ref[...], va(R_PRE_MIX)).astype(bf16)
        hn = lambda: hn_val
    else:
        hn = lambda: prenorm[0][...]
    if not streaming:
        for s in range(nseg):
            xbc_ext[s, HDR - (CONV_W - 1):HDR, :] = conv0_ref[s]
            pool_ext[s, 0:HDR - POOL_HIST, :] = jnp.zeros((HDR - POOL_HIST, D_POOL), f32)
            pool_ext[s, HDR - POOL_HIST:HDR, :] = pool0_ref[s]

    dts = []
    for g in range(SSM_GROUPS):
        yield 'x', X_PIECE * k
        dts.append(_dot(hn(), w_in_ref[:, DT0 + g * W:DT0 + (g + 1) * W]))
    for j in range(CONV_DIM // W):
        yield 'x', X_PIECE * k
        piece = _dot(hn(), w_in_ref[:, XBC0 + j * W:XBC0 + (j + 1) * W])
        for s in range(nseg):
            xbc_ext[s, HDR:HDR + L, j * W:(j + 1) * W] = piece[seg_rows[s]]
    for s in range(nseg):
        conv_out_ref[s] = xbc_ext[s, HDR + L - (CONV_W - 1):HDR + L, :]

    acums = []
    for g in range(SSM_GROUPS):
        dt_halves, acum_halves = [], []
        for l0 in range(g * W, (g + 1) * W, H):
            yield 'v', 170 * k
            bias = _take_tokens(out.setdefault('finish_done', []), vb(R_DT_BIAS)[:, l0:l0 + H])
            d = dts[g][:, l0 - g * W:l0 - g * W + H] + bias
            d = jnp.maximum(d, 0.0) + jnp.log(1.0 + jnp.exp(-jnp.abs(d)))
            dt_halves.append(d)
            acum_halves.append(_chunk_cumsum(d * -jnp.exp(vb(R_A_LOG)[:, l0:l0 + H])))
        dts[g] = jnp.concatenate(dt_halves, axis=1)
        acums.append(jnp.concatenate(acum_halves, axis=1))

    xbc = []
    for j in range(CONV_DIM // H):
        yield 'v', 200 * k
        ls = slice(j * H, (j + 1) * H)
        parts = []
        for s in range(nseg):
            acc = va(R_CONV_B)[:, ls]
            for kk in range(CONV_W):
                lo = HDR - (CONV_W - 1) + kk
                acc = acc + va(R_CONV_W0 + kk)[:, ls] * xbc_ext[s, lo:lo + L, ls]
            parts.append(acc)
        piece = parts[0] if nseg == 1 else jnp.concatenate(parts, axis=0)
        xbc.append(piece * _sigmoid(piece))
    xbc = [jnp.concatenate(xbc[j:j + W // H], axis=1) for j in range(0, len(xbc), W // H)]
    xs, bmat, cmat = xbc[0:SSM_GROUPS], xbc[SSM_GROUPS], xbc[SSM_GROUPS + 1]
    if streaming:
        xbc_ext[:, 0:HDR, :] = xbc_ext[:, L:L + HDR, :]

    zs = []
    for g in range(SSM_GROUPS):
        yield 'x', X_PIECE * k
        zs.append(_dot(hn(), w_in_ref[:, Z0 + g * W:Z0 + (g + 1) * W]))
    for j in range(D_POOL // W):
        yield 'x', X_PIECE * k
        piece = _dot(hn(), w_in_ref[:, U0 + j * W:U0 + (j + 1) * W])
        for s in range(nseg):
            pool_ext[s, HDR:HDR + L, j * W:(j + 1) * W] = piece[seg_rows[s]]
    for s in range(nseg):
        pool_out_ref[s] = pool_ext[s, L + HDR - POOL_HIST:L + HDR, :]
    if prenorm is not None:
        for rows in row_pieces:
            yield 'v', 160
            hn_next = _rms(prenorm[1][rows], va(R_PRE_MIX))
            prenorm[0][rows] = hn_next.astype(bf16)
            prenorm_done.append(_done_token(hn_next))

    assert 2 * SSM_HEAD_DIM == LANES and Q == SSM_HEAD_DIM
    lane = lax.broadcasted_iota(jnp.int32, (Q, LANES), 1)
    rowq = lax.broadcasted_iota(jnp.int32, (Q, LANES), 0)
    eye2 = rowq == (lane % Q)
    causal2 = rowq >= (lane % Q)
    row2 = lax.broadcasted_iota(jnp.int32, (2 * Q, LANES), 0)
    lane2 = lax.broadcasted_iota(jnp.int32, (2 * Q, LANES), 1)
    blockmask = (row2 // Q) == (lane2 // SSM_HEAD_DIM)
    masks = (eye2, causal2, blockmask)

    for s in range(nseg):
        states = []
        for g in range(SSM_GROUPS):
            if streaming:
                states.append(st_ref[g])
            else:
                states.append(ssm0_ref[s, g * HPG:(g + 1) * HPG].reshape(W, D_STATE).T)
        for c in range(L // Q):
            rs = slice(s * L + c * Q, s * L + (c + 1) * Q)
            for g in range(SSM_GROUPS):
                yield 'v', 200
                bg = bmat[rs, g * D_STATE:(g + 1) * D_STATE]
                cg = cmat[rs, g * D_STATE:(g + 1) * D_STATE]
                yg, states[g] = _ssd_block(xs[g][rs], dts[g][rs], acums[g][rs], bg, cg, states[g], masks)
                y_scr[rs, g * W:(g + 1) * W] = yg
        for g in range(SSM_GROUPS):
            if streaming:
                st_ref[g] = states[g]
            else:
                ssm_out_ref[s, g * HPG:(g + 1) * HPG] = states[g].T.reshape(HPG, SSM_HEAD_DIM, D_STATE)

    ys = []
    for g in range(SSM_GROUPS):
        gl = slice(g * W, (g + 1) * W)
        parts = []
        for rows in row_pieces:
            yield 'v', 130
            d_skip = _take_tokens(prenorm_done, vb(R_D_SKIP)[:, gl])
            yg = y_scr[rows, gl] + d_skip * xs[g][rows]
            yg = yg * (zs[g][rows] * _sigmoid(zs[g][rows]))
            yg = yg * lax.rsqrt(jnp.mean(yg * yg, axis=-1, keepdims=True) + EPS)
            parts.append((yg * vb(R_SSM_G)[:, gl]).astype(bf16))
        ys.append(jnp.concatenate(parts, axis=0))

    pos0 = t * L if streaming else PAST_LEN
    pos1 = pos0 + 1 + lax.broadcasted_iota(jnp.int32, (L, POOL_GROUP), 0)
    assert POOL_WINDOWS == tuple(2 ** (i + 1) for i in range(len(POOL_WINDOWS))) and HDR > POOL_HIST
    pooled = []
    sums = None
    for gi, w in enumerate(POOL_WINDOWS):
        yield 'v', (150 - 30 * gi) * k
        if sums is None:
            sums = [pool_ext[s] for s in range(nseg)]
        sums = [x + pltpu.roll(x, w // 2, axis=0) for x in sums]
        cnt = jnp.minimum(pos1, w).astype(f32)
        ls = slice(gi * POOL_GROUP, (gi + 1) * POOL_GROUP)
        parts = [x[HDR:HDR + L, 0:POOL_GROUP] / cnt - pool_ext[s, HDR:HDR + L, ls]
                 for s, x in enumerate(sums)]
        piece = parts[0] if nseg == 1 else jnp.concatenate(parts, axis=0)
        pooled.append(piece.astype(bf16))
        sums = [x[:, POOL_GROUP:] for x in sums]
    if streaming:
        pool_ext[:, 0:HDR, :] = pool_ext[:, L:L + HDR, :]

    yield 'x', X_PIECE // 2 * k
    pool_o = jnp.concatenate([_dot(jnp.concatenate(pooled[0:2], axis=1), pool_w_ref[0]),
                              _dot(jnp.concatenate(pooled[2:4], axis=1), pool_w_ref[1])], axis=1)
    pool_o = ((pool_o + vb(R_POOL_B)) * vb(R_POOL_SCALE)).astype(bf16)
    mixer_out = jnp.concatenate(ys + [pool_o], axis=1)
    mix = []
    for j in range(D_MODEL // MXU_TILE):
        yield 'x', X_PIECE * k
        mix.append(_dot(mixer_out, w_out_ref[:, j * MXU_TILE:(j + 1) * MXU_TILE]))
    mix = jnp.concatenate(mix, axis=1)
    h1_parts, hn1_parts = [], []
    for rows in row_pieces:
        yield 'v', 300, LATE
        h1 = h_ref[rows] + _rms(mix[rows], va(R_POST_MIX))
        h1_parts.append(h1)
        hn1_parts.append(_rms(h1, va(R_PRE_FFN)).astype(bf16))
    out['h1'] = jnp.concatenate(h1_parts, axis=0)
    out['hn1'] = jnp.concatenate(hn1_parts, axis=0)


def _ple_finish(h_out_ref, h_ref, e_ref, gate_ref, va, done=None):
    for r in range(0, h_ref.shape[0], ROW_PIECE):
        yield 'v', 300
        rows = slice(r, r + ROW_PIECE)
        res = h_ref[rows] + _rms(e_ref[rows] * _sigmoid(gate_ref[rows]), va(R_PLE))
        h_out_ref[rows] = res
        if done is not None:
            done.append(_done_token(res))


def _ffn_ple_steps(h_out_ref, h1_ref, hn1_ref, p_ref, va, w_gate_ref, w_up_ref, w_down_ref, w_pg_ref,
                   w_pp_ref, scratch, *, deferred, finish_done=None):
    act_scr, f_scr, gate_scr, h2_scr, e_scr = scratch
    k = max(1, h1_ref.shape[0] // COST_ROWS)
    if deferred:
        yield from _ple_finish(h_out_ref, h2_scr, e_scr, gate_scr, va, finish_done)
    nout = D_MODEL // MXU_TILE
    for bi, (lo, hi) in enumerate(FF_BLOCKS):
        for c0 in range(lo, hi, MXU_TILE):
            yield 'x', X_PIECE * k
            gt = _dot(hn1_ref[...], w_gate_ref[:, c0:c0 + MXU_TILE])
            yield 'x', X_PIECE * k
            up = _dot(hn1_ref[...], w_up_ref[:, c0:c0 + MXU_TILE])
            yield 'v', 130 * k
            act_scr[:, c0 - lo:c0 - lo + MXU_TILE] = (gt * _sigmoid(gt) * up).astype(bf16)
        for j in range(nout):
            yield 'x', (hi - lo) // 4 * k
            cols = slice(j * MXU_TILE, (j + 1) * MXU_TILE)
            part = _dot(act_scr[:, 0:hi - lo], w_down_ref[lo:hi, cols])
            f_scr[:, cols] = part if bi == 0 else f_scr[:, cols] + part
    hb = []
    for r in range(0, h1_ref.shape[0], ROW_PIECE):
        yield 'v', 150
        rows = slice(r, r + ROW_PIECE)
        h = h1_ref[rows] + _rms(f_scr[rows], va(R_POST_FFN))
        h2_scr[rows] = h
        hb.append(h.astype(bf16))
    hb = jnp.concatenate(hb, axis=0)
    for j in range(nout):
        yield 'x', X_PIECE * k, LATE
        gate_scr[:, j * MXU_TILE:(j + 1) * MXU_TILE] = _dot(hb, w_pg_ref[:, j * MXU_TILE:(j + 1) * MXU_TILE])
    yield 'x', X_PIECE * k, LATE
    e_scr[...] = _dot(p_ref[...].astype(bf16), w_pp_ref[...])
    if not deferred:
        yield from _ple_finish(h_out_ref, h2_scr, e_scr, gate_scr, va)


def _merge(lead, fill=None):
    spent = {'x': 0.0, 'v': 0.0}

    def advance(stream, head):
        spent[head[0]] += head[1]
        return next(stream, None)

    lead_head = next(lead, None)
    fill_head = next(fill, None) if fill is not None else None
    while lead_head is not None:
        while (fill_head is not None and (LATE in lead_head or LATE not in fill_head)
               and spent['x'] <= MATRIX_TO_VECTOR_TIME * spent['v']):
            fill_head = advance(fill, fill_head)
        lead_head = advance(lead, lead_head)
    while fill_head is not None:
        fill_head = advance(fill, fill_head)


def _row_getters(va_ref, vb_ref):
    return (lambda r: va_ref[r:r + 1, :]), (lambda r: vb_ref[r:r + 1, :])


def _prompt_kernel(h_ref, h_next_ref, p_ref, va_ref, vb_ref, w_in_ref, pool_w_ref, w_out_ref, w_gate_ref,
                   w_up_ref, w_down_ref, w_pg_ref, w_pp_ref,
                   h_out_ref, ssm_out_ref, conv_out_ref, pool_out_ref,
                   xbc_ext, pool_ext, st_ref, y_scr, h1_scr, hn1_scr, act_scr, f_scr, gate_scr, h2_scr, e_scr,
                   hn_scr, *, tiles_per_seq, tile):
    n = pl.program_id(0)
    t = n % tiles_per_seq
    va, vb = _row_getters(va_ref, vb_ref)

    @pl.when(n == 0)
    def _():
        for ref in (h1_scr, hn1_scr, f_scr, gate_scr, h2_scr, e_scr):
            ref[...] = jnp.zeros(ref.shape, ref.dtype)
        hn_scr[...] = _rms(h_ref[...], va(R_PRE_MIX)).astype(bf16)

    @pl.when(t == 0)
    def _():
        xbc_ext[:, 0:HDR, :] = jnp.zeros((1, HDR, CONV_DIM), f32)
        pool_ext[:, 0:HDR, :] = jnp.zeros((1, HDR, D_POOL), f32)
        st_ref[...] = jnp.zeros(st_ref.shape, f32)

    out = {'finish_done': []}
    mixer = _mixer_steps(out, h_ref, (None, None, None), va, vb, w_in_ref, pool_w_ref, w_out_ref,
                         (None, conv_out_ref, pool_out_ref), (xbc_ext, pool_ext, st_ref, y_scr),
                         nseg=1, seg_len=tile, t=t, prenorm=(hn_scr, h_next_ref))
    ffn = _ffn_ple_steps(h_out_ref, h1_scr, hn1_scr, p_ref, va, w_gate_ref, w_up_ref, w_down_ref, w_pg_ref,
                         w_pp_ref, (act_scr, f_scr, gate_scr, h2_scr, e_scr), deferred=True,
                         finish_done=out['finish_done'])
    _merge(mixer, ffn)
    h1_scr[...] = out['h1']
    hn1_scr[...] = out['hn1']

    @pl.when(t == tiles_per_seq - 1)
    def _():
        for g in range(SSM_GROUPS):
            ssm_out_ref[g * HPG:(g + 1) * HPG] = st_ref[g].T.reshape(HPG, SSM_HEAD_DIM, D_STATE)


def _sample_kernel(h_ref, p_ref, ssm0_ref, conv0_ref, pool0_ref, va_ref, vb_ref, w_in_ref, pool_w_ref,
                   w_out_ref, w_gate_ref, w_up_ref, w_down_ref, w_pg_ref, w_pp_ref,
                   h_out_ref, ssm_out_ref, conv_out_ref, pool_out_ref,
                   xbc_ext, pool_ext, st_ref, y_scr, h1_scr, hn1_scr, act_scr, f_scr, gate_scr, h2_scr, e_scr,
                   *, nseg, seg_len):
    va, vb = _row_getters(va_ref, vb_ref)
    out = {}
    _merge(_mixer_steps(out, h_ref, (ssm0_ref, conv0_ref, pool0_ref), va, vb, w_in_ref, pool_w_ref,
                        w_out_ref, (ssm_out_ref, conv_out_ref, pool_out_ref),
                        (xbc_ext, pool_ext, st_ref, y_scr), nseg=nseg, seg_len=seg_len, t=None))
    h1_scr[...] = out['h1']
    hn1_scr[...] = out['hn1']
    _merge(_ffn_ple_steps(h_out_ref, h1_scr, hn1_scr, p_ref, va, w_gate_ref, w_up_ref, w_down_ref, w_pg_ref,
                          w_pp_ref, (act_scr, f_scr, gate_scr, h2_scr, e_scr), deferred=False))


def _resident(shape_tail, layer):
    nd = len(shape_tail)
    return pl.BlockSpec((None,) + tuple(shape_tail), lambda *_: (layer,) + (0,) * nd,
                        pipeline_mode=pl.Buffered(1))


def _weight_specs(layer):
    return [
        _resident((16, D_MODEL), layer),
        _resident((8, D_SSM), layer),
        _resident((D_MODEL, IN_EXT), layer),
        _resident((2, MXU_TILE, MXU_TILE), layer),
        _resident((D_MODEL, D_MODEL), layer),
        _resident((D_MODEL, D_FF), layer),
        _resident((D_MODEL, D_FF), layer),
        _resident((D_FF, D_MODEL), layer),
        _resident((D_MODEL, D_MODEL), layer),
        _resident((D_PLE, D_MODEL), layer),
    ]


def _scratch(nseg, seg_len, pipelined):
    m = nseg * seg_len
    pre = [pltpu.VMEM((m, D_MODEL), bf16)] if pipelined else []
    return [
        pltpu.VMEM((nseg, HDR + seg_len, CONV_DIM), f32),
        pltpu.VMEM((nseg, HDR + seg_len, D_POOL), f32),
        pltpu.VMEM((SSM_GROUPS, D_STATE, GROUP_W), f32),
        pltpu.VMEM((m, D_SSM), f32),
        pltpu.VMEM((m, D_MODEL), f32),
        pltpu.VMEM((m, D_MODEL), bf16),
        pltpu.VMEM((m, max(hi - lo for lo, hi in FF_BLOCKS)), bf16),
        pltpu.VMEM((m, D_MODEL), f32),
        pltpu.VMEM((m, D_MODEL), f32),
        pltpu.VMEM((m, D_MODEL), f32),
        pltpu.VMEM((m, D_MODEL), f32),
    ] + pre


def _prompt_layer(layer, h, p_all, weights):
    nb, nl, _ = h.shape
    T = PROMPT_TILE
    assert nl % T == 0 and T % Q == 0
    nt = nl // T
    ntiles = nb * nt

    def tile_at(offset):
        def index(n):
            m = jnp.clip(n + offset, 0, ntiles - 1)
            return m // nt, m % nt
        return index

    nxt, cur, prev, prev2 = tile_at(1), tile_at(0), tile_at(-1), tile_at(-2)

    in_specs = [
        pl.BlockSpec((None, T, D_MODEL), lambda n: (*cur(n), 0)),
        pl.BlockSpec((None, T, D_MODEL), lambda n: (*nxt(n), 0)),
        pl.BlockSpec((None, None, T, D_PLE), lambda n: (layer, *prev(n), 0)),
    ] + _weight_specs(layer)
    out_shape = (
        jax.ShapeDtypeStruct((nb, nl, D_MODEL), f32),
        jax.ShapeDtypeStruct((nb, SSM_HEADS, SSM_HEAD_DIM, D_STATE), f32),
        jax.ShapeDtypeStruct((nb, 1, CONV_W - 1, CONV_DIM), f32),
        jax.ShapeDtypeStruct((nb, 1, POOL_HIST, D_POOL), f32),
    )
    out_specs = (
        pl.BlockSpec((None, T, D_MODEL), lambda n: (*prev2(n), 0)),
        pl.BlockSpec((None, SSM_HEADS, SSM_HEAD_DIM, D_STATE), lambda n: (cur(n)[0], 0, 0, 0)),
        pl.BlockSpec((None, 1, CONV_W - 1, CONV_DIM), lambda n: (cur(n)[0], 0, 0, 0)),
        pl.BlockSpec((None, 1, POOL_HIST, D_POOL), lambda n: (cur(n)[0], 0, 0, 0)),
    )
    h_new, ssm, conv, pool = pl.pallas_call(
        functools.partial(_prompt_kernel, tiles_per_seq=nt, tile=T),
        grid=(ntiles + 2,), in_specs=in_specs, out_specs=out_specs, out_shape=out_shape,
        scratch_shapes=_scratch(1, T, True),
        compiler_params=pltpu.CompilerParams(
            dimension_semantics=("arbitrary",), vmem_limit_bytes=VMEM_LIMIT),
        name=f"prompt_layer{layer}",
    )(h, h, p_all, *weights)
    return h_new, ssm, conv[:, 0], pool[:, 0]


def _sample_layer(layer, h2d, p_all, ssm_all, conv_all, pool_all, weights, nb, nl):
    assert nl % Q == 0
    M = nb * nl
    full = lambda *tail: pl.BlockSpec((None,) + tail, lambda i: (layer,) + (0,) * len(tail))
    in_specs = [
        pl.BlockSpec((M, D_MODEL), lambda i: (0, 0)),
        full(M, D_PLE),
        full(nb, SSM_HEADS, SSM_HEAD_DIM, D_STATE),
        full(nb, CONV_W - 1, CONV_DIM),
        full(nb, POOL_HIST, D_POOL),
    ] + _weight_specs(layer)
    out_shape = (
        jax.ShapeDtypeStruct((M, D_MODEL), f32),
        jax.ShapeDtypeStruct((nb, SSM_HEADS, SSM_HEAD_DIM, D_STATE), f32),
        jax.ShapeDtypeStruct((nb, CONV_W - 1, CONV_DIM), f32),
        jax.ShapeDtypeStruct((nb, POOL_HIST, D_POOL), f32),
    )
    out_specs = tuple(pl.BlockSpec(s.shape, lambda i, n=len(s.shape): (0,) * n) for s in out_shape)
    return pl.pallas_call(
        functools.partial(_sample_kernel, nseg=nb, seg_len=nl),
        grid=(1,), in_specs=in_specs, out_specs=out_specs, out_shape=out_shape,
        scratch_shapes=_scratch(nb, nl, False),
        compiler_params=pltpu.CompilerParams(
            dimension_semantics=("arbitrary",), vmem_limit_bytes=VMEM_LIMIT),
        name=f"sample_layer{layer}",
    )(h2d, p_all, ssm_all, conv_all, pool_all, *weights)


def _pack_weights(pre_mix_g, w_in, conv_w, conv_b, dt_bias, a_log, d_skip, ssm_norm_g, pool_w, pool_b,
                  pool_scale, w_out, post_mix_g, pre_ffn_g, w_gate, w_up, w_down, post_ffn_g, w_ple_gate,
                  w_ple_proj, ple_norm_g):
    dt_cols = D_SSM + CONV_DIM
    w_in = w_in.astype(bf16)
    w_in_ext = jnp.concatenate([
        w_in[:, :, 0:dt_cols],
        w_in[:, :, dt_cols + SSM_HEADS:],
        jnp.repeat(w_in[:, :, dt_cols:dt_cols + SSM_HEADS], SSM_HEAD_DIM, axis=2),
    ], axis=2)
    rep = lambda v: jnp.repeat(v, SSM_HEAD_DIM, axis=1)
    va = jnp.stack([pre_mix_g, post_mix_g, pre_ffn_g, post_ffn_g, ple_norm_g, conv_b]
                   + [conv_w[:, k] for k in range(CONV_W)], axis=1)
    va = jnp.pad(va, ((0, 0), (0, 16 - va.shape[1]), (0, 0)))
    vb = jnp.stack([rep(dt_bias), rep(a_log), rep(d_skip), ssm_norm_g,
                    pool_b.reshape(DEPTH, D_POOL), pool_scale], axis=1)
    vb = jnp.pad(vb, ((0, 0), (0, 8 - vb.shape[1]), (0, 0)))
    zeros = jnp.zeros((DEPTH, POOL_GROUP, POOL_GROUP), f32)
    pw = [jnp.concatenate([jnp.concatenate([pool_w[:, 2 * k], zeros], axis=2),
                           jnp.concatenate([zeros, pool_w[:, 2 * k + 1]], axis=2)], axis=1)
          for k in range(2)]
    pool_w2 = jnp.stack(pw, axis=1).astype(bf16)
    return (va, vb, w_in_ext, pool_w2, w_out.astype(bf16), w_gate.astype(bf16), w_up.astype(bf16),
            w_down.astype(bf16), w_ple_gate.astype(bf16), w_ple_proj.astype(bf16))


def kernel(x_prompt, x_sample, state_ssm, state_conv, state_pool, p_prompt, p_sample, pre_mix_g, w_in, conv_w, conv_b, dt_bias, a_log, d_skip, ssm_norm_g, pool_w, pool_b, pool_scale, w_out, post_mix_g, pre_ffn_g, w_gate, w_up, w_down, post_ffn_g, w_ple_gate, w_ple_proj, ple_norm_g):
    weights = _pack_weights(pre_mix_g, w_in, conv_w, conv_b, dt_bias, a_log, d_skip, ssm_norm_g, pool_w,
                            pool_b, pool_scale, w_out, post_mix_g, pre_ffn_g, w_gate, w_up, w_down,
                            post_ffn_g, w_ple_gate, w_ple_proj, ple_norm_g)
    nbs, nls, _ = x_sample.shape
    hp = x_prompt
    hs = x_sample.reshape(nbs * nls, D_MODEL)
    p_sample2d = p_sample.reshape(DEPTH, nbs * nls, D_PLE)
    outs_p, outs_s = [], []
    for i in range(DEPTH):
        hp, ssm, conv, pool = _prompt_layer(i, hp, p_prompt, weights)
        outs_p.append((ssm, conv, pool))
        hs, ssm, conv, pool = _sample_layer(i, hs, p_sample2d, state_ssm, state_conv, state_pool,
                                            weights, nbs, nls)
        outs_s.append((ssm, conv, pool))
    stack = lambda outs, j: jnp.stack([o[j] for o in outs])
    return (hp, hs.reshape(nbs, nls, D_MODEL),
            stack(outs_p, 0), stack(outs_p, 1), stack(outs_p, 2),
            stack(outs_s, 0), stack(outs_s, 1), stack(outs_s, 2))
```

```python
import functools

import jax
import jax.numpy as jnp
from jax import lax
from jax.experimental import pallas as pl
from jax.experimental.pallas import tpu as pltpu

D_MODEL = 1024
DEPTH = 4
PAST_LEN = 4096
D_SSM = 512
D_POOL = 512
SSM_HEAD_DIM = 64
SSM_HEADS = 8
SSM_GROUPS = 2
HPG = SSM_HEADS // SSM_GROUPS
D_STATE = 128
CONV_W = 4
CONV_DIM = 1024
POOL_WINDOWS = (2, 4, 8, 16)
POOL_GROUP = 128
POOL_HIST = 15
D_FF = 2816
D_PLE = 256
EPS = 1e-6

Q = 64
GROUP_W = D_SSM // SSM_GROUPS
HDR = 16
PROMPT_TILE = 256
FF_BLOCKS = ((0, 1024), (1024, 2048), (2048, D_FF))
NEG = -1e30
LANES, SUBLANES = 128, 8
MXU_TILE = 256
COST_ROWS = 256
X_PIECE = 256
LATE = 'late'
ROW_PIECE = 128
MATRIX_TO_VECTOR_TIME = 1.6
VMEM_CAPACITY_V7X = 64 * 1024 * 1024
VMEM_LIMIT = VMEM_CAPACITY_V7X - 8 * 1024 * 1024

Z0, XBC0, ZX_COLS = 0, 512, 1536

(R_PRE_MIX, R_POST_MIX, R_PRE_FFN, R_POST_FFN, R_PLE, R_CONV_B, R_CONV_W0) = range(7)
(R_DT_BIAS, R_A_LOG, R_D_SKIP, R_SSM_G, R_POOL_B, R_POOL_SCALE) = range(6)

f32 = jnp.float32
bf16 = jnp.bfloat16


def _rms(x, g_row):
    ms = jnp.mean(x * x, axis=-1, keepdims=True)
    return x * lax.rsqrt(ms + EPS) * g_row


def _sigmoid(x):
    return 1.0 / (1.0 + jnp.exp(-x))


def _dot(a, b):
    return jnp.dot(a, b, preferred_element_type=f32)


def _done_token(x):
    bits = pltpu.bitcast(x, jnp.uint32)
    acc = bits[:, 0:LANES]
    for l in range(LANES, x.shape[1], LANES):
        acc = acc | bits[:, l:l + LANES]
    rows = acc[0:SUBLANES]
    for r in range(SUBLANES, x.shape[0], SUBLANES):
        rows = rows | acc[r:r + SUBLANES]
    zero = lax.shift_right_logical(lax.shift_right_logical(rows, jnp.uint32(16)), jnp.uint32(16))
    return zero[0:1].astype(f32)


def _take_tokens(tokens, row):
    while tokens:
        row = row + jnp.concatenate([tokens.pop()] * (row.shape[1] // LANES), axis=1)
    return row


def _chunk_cumsum(x):
    row = lax.broadcasted_iota(jnp.int32, x.shape, 0) % Q
    sh = 1
    while sh < Q:
        x = x + jnp.where(row >= sh, pltpu.roll(x, sh, axis=0), 0.0)
        sh *= 2
    return x


def _ssd_block(xg, dtg, ag, bg, cg, s, masks):
    eye2, causal2, blockmask = masks
    a_last = ag[Q - 1:Q, :]
    xdt = xg * dtg
    bb = bg.astype(bf16)
    cb_ = cg.astype(bf16)
    b2 = jnp.concatenate([bb, bb], axis=0)
    cb2 = lax.dot_general(cb_, b2, (((1,), (1,)), ((), ())),
                          preferred_element_type=f32)
    ys = []
    for q in range(2):
        p_col = ag[:, q * LANES:(q + 1) * LANES]
        p_row = jnp.sum(jnp.where(eye2, p_col, 0.0), axis=0, keepdims=True)
        lmat = jnp.exp(jnp.where(causal2, p_col - p_row, NEG))
        g = (cb2 * lmat).astype(bf16)
        xp = xdt[:, q * LANES:(q + 1) * LANES]
        bd = jnp.where(blockmask, jnp.concatenate([xp, xp], axis=0), 0.0).astype(bf16)
        ys.append(_dot(g, bd))
    y_diag = jnp.concatenate(ys, axis=1)
    y_off = _dot(cb_, s.astype(bf16)) * jnp.exp(ag)
    xdtd = (xdt * jnp.exp(a_last - ag)).astype(bf16)
    states = _dot(bg.T.astype(bf16), xdtd)
    s_new = s * jnp.exp(a_last) + states
    return y_diag + y_off, s_new


def _mixer_steps(out, h_ref, state_in, va, vb, w_in_ref, pool_w_ref, w_out_ref, state_out, scratch,
                 *, nseg, seg_len, t, prenorm=None):
    streaming = t is not None
    w_zx_ref, w_u_ref, w_dt_ref = w_in_ref
    ssm0_ref, conv0_ref, pool0_ref = state_in
    ssm_out_ref, conv_out_ref, pool_out_ref = state_out
    xbc_ext, pool_ext, st_ref, y_scr = scratch
    L = seg_len
    k = max(1, nseg * L // COST_ROWS)
    W = GROUP_W
    seg_rows = [slice(s * L, (s + 1) * L) for s in range(nseg)]
    row_pieces = [slice(r, r + ROW_PIECE) for r in range(0, nseg * L, ROW_PIECE)]
    H = W // 2
    prenorm_done = []

    if prenorm is None:
        yield 'v', 320 * k
        hn_val = _rms(h_ref[...], va(R_PRE_MIX)).astype(bf16)
        hn = lambda: hn_val
    else:
        hn = lambda: prenorm[0][...]
    if not streaming:
        for s in range(nseg):
            xbc_ext[s, HDR - (CONV_W - 1):HDR, :] = conv0_ref[s]
            pool_ext[s, 0:HDR - POOL_HIST, :] = jnp.zeros((HDR - POOL_HIST, D_POOL), f32)
            pool_ext[s, HDR - POOL_HIST:HDR, :] = pool0_ref[s]

    dts = []
    for g in range(SSM_GROUPS):
        yield 'x', X_PIECE * k
        dts.append(_dot(hn(), w_dt_ref[:, g * W:(g + 1) * W]))
    for j in range(CONV_DIM // W):
        yield 'x', X_PIECE * k
        piece = _dot(hn(), w_zx_ref[:, XBC0 + j * W:XBC0 + (j + 1) * W])
        for s in range(nseg):
            xbc_ext[s, HDR:HDR + L, j * W:(j + 1) * W] = piece[seg_rows[s]]
    for s in range(nseg):
        conv_out_ref[s] = xbc_ext[s, HDR + L - (CONV_W - 1):HDR + L, :]

    acums = []
    for g in range(SSM_GROUPS):
        dt_halves, acum_halves = [], []
        for l0 in range(g * W, (g + 1) * W, H):
            yield 'v', 170 * k
            bias = _take_tokens(out.setdefault('finish_done', []), vb(R_DT_BIAS)[:, l0:l0 + H])
            d = dts[g][:, l0 - g * W:l0 - g * W + H] + bias
            d = jnp.maximum(d, 0.0) + jnp.log(1.0 + jnp.exp(-jnp.abs(d)))
            dt_halves.append(d)
            acum_halves.append(_chunk_cumsum(d * -jnp.exp(vb(R_A_LOG)[:, l0:l0 + H])))
        dts[g] = jnp.concatenate(dt_halves, axis=1)
        acums.append(jnp.concatenate(acum_halves, axis=1))

    xbc = []
    for j in range(CONV_DIM // H):
        yield 'v', 200 * k
        ls = slice(j * H, (j + 1) * H)
        parts = []
        for s in range(nseg):
            acc = va(R_CONV_B)[:, ls]
            for kk in range(CONV_W):
                lo = HDR - (CONV_W - 1) + kk
                acc = acc + va(R_CONV_W0 + kk)[:, ls] * xbc_ext[s, lo:lo + L, ls]
            parts.append(acc)
        piece = parts[0] if nseg == 1 else jnp.concatenate(parts, axis=0)
        xbc.append(piece * _sigmoid(piece))
    xbc = [jnp.concatenate(xbc[j:j + W // H], axis=1) for j in range(0, len(xbc), W // H)]
    xs, bmat, cmat = xbc[0:SSM_GROUPS], xbc[SSM_GROUPS], xbc[SSM_GROUPS + 1]
    if streaming:
        xbc_ext[:, 0:HDR, :] = xbc_ext[:, L:L + HDR, :]

    zs = []
    for g in range(SSM_GROUPS):
        yield 'x', X_PIECE * k
        zs.append(_dot(hn(), w_zx_ref[:, Z0 + g * W:Z0 + (g + 1) * W]))
    for j in range(D_POOL // W):
        yield 'x', X_PIECE * k
        piece = _dot(hn(), w_u_ref[:, j * W:(j + 1) * W])
        for s in range(nseg):
            pool_ext[s, HDR:HDR + L, j * W:(j + 1) * W] = piece[seg_rows[s]]
    for s in range(nseg):
        pool_out_ref[s] = pool_ext[s, L + HDR - POOL_HIST:L + HDR, :]
    if prenorm is not None:
        for rows in row_pieces:
            yield 'v', 160
            hn_next = _rms(prenorm[1][rows], va(R_PRE_MIX))
            prenorm[0][rows] = hn_next.astype(bf16)
            prenorm_done.append(_done_token(hn_next))

    assert 2 * SSM_HEAD_DIM == LANES and Q == SSM_HEAD_DIM
    lane = lax.broadcasted_iota(jnp.int32, (Q, LANES), 1)
    rowq = lax.broadcasted_iota(jnp.int32, (Q, LANES), 0)
    eye2 = rowq == (lane % Q)
    causal2 = rowq >= (lane % Q)
    row2 = lax.broadcasted_iota(jnp.int32, (2 * Q, LANES), 0)
    lane2 = lax.broadcasted_iota(jnp.int32, (2 * Q, LANES), 1)
    blockmask = (row2 // Q) == (lane2 // SSM_HEAD_DIM)
    masks = (eye2, causal2, blockmask)

    for s in range(nseg):
        states = []
        for g in range(SSM_GROUPS):
            if streaming:
                states.append(st_ref[g])
            else:
                states.append(ssm0_ref[s, g * HPG:(g + 1) * HPG].reshape(W, D_STATE).T)
        for c in range(L // Q):
            rs = slice(s * L + c * Q, s * L + (c + 1) * Q)
            for g in range(SSM_GROUPS):
                yield 'v', 200
                bg = bmat[rs, g * D_STATE:(g + 1) * D_STATE]
                cg = cmat[rs, g * D_STATE:(g + 1) * D_STATE]
                yg, states[g] = _ssd_block(xs[g][rs], dts[g][rs], acums[g][rs], bg, cg, states[g], masks)
                y_scr[rs, g * W:(g + 1) * W] = yg
        for g in range(SSM_GROUPS):
            if streaming:
                st_ref[g] = states[g]
            else:
                ssm_out_ref[s, g * HPG:(g + 1) * HPG] = states[g].T.reshape(HPG, SSM_HEAD_DIM, D_STATE)

    ys = []
    for g in range(SSM_GROUPS):
        gl = slice(g * W, (g + 1) * W)
        parts = []
        for rows in row_pieces:
            yield 'v', 130
            d_skip = _take_tokens(prenorm_done, vb(R_D_SKIP)[:, gl])
            yg = y_scr[rows, gl] + d_skip * xs[g][rows]
            yg = yg * (zs[g][rows] * _sigmoid(zs[g][rows]))
            yg = yg * lax.rsqrt(jnp.mean(yg * yg, axis=-1, keepdims=True) + EPS)
            parts.append((yg * vb(R_SSM_G)[:, gl]).astype(bf16))
        ys.append(jnp.concatenate(parts, axis=0))

    pos0 = t * L if streaming else PAST_LEN
    pos1 = pos0 + 1 + lax.broadcasted_iota(jnp.int32, (L, POOL_GROUP), 0)
    assert POOL_WINDOWS == tuple(2 ** (i + 1) for i in range(len(POOL_WINDOWS))) and HDR > POOL_HIST
    pooled = []
    sums = None
    for gi, w in enumerate(POOL_WINDOWS):
        yield 'v', (150 - 30 * gi) * k
        if sums is None:
            sums = [pool_ext[s] for s in range(nseg)]
        sums = [x + pltpu.roll(x, w // 2, axis=0) for x in sums]
        cnt = jnp.minimum(pos1, w).astype(f32)
        ls = slice(gi * POOL_GROUP, (gi + 1) * POOL_GROUP)
        parts = [x[HDR:HDR + L, 0:POOL_GROUP] / cnt - pool_ext[s, HDR:HDR + L, ls]
                 for s, x in enumerate(sums)]
        piece = parts[0] if nseg == 1 else jnp.concatenate(parts, axis=0)
        pooled.append(piece.astype(bf16))
        sums = [x[:, POOL_GROUP:] for x in sums]
    if streaming:
        pool_ext[:, 0:HDR, :] = pool_ext[:, L:L + HDR, :]

    yield 'x', X_PIECE // 2 * k
    pool_o = jnp.concatenate([_dot(jnp.concatenate(pooled[0:2], axis=1), pool_w_ref[0]),
                              _dot(jnp.concatenate(pooled[2:4], axis=1), pool_w_ref[1])], axis=1)
    pool_o = ((pool_o + vb(R_POOL_B)) * vb(R_POOL_SCALE)).astype(bf16)
    mixer_out = jnp.concatenate(ys + [pool_o], axis=1)
    mix = []
    for j in range(D_MODEL // MXU_TILE):
        yield 'x', X_PIECE * k
        mix.append(_dot(mixer_out, w_out_ref[:, j * MXU_TILE:(j + 1) * MXU_TILE]))
    mix = jnp.concatenate(mix, axis=1)
    h1_parts, hn1_parts = [], []
    for rows in row_pieces:
        yield 'v', 300, LATE
        h1 = h_ref[rows] + _rms(mix[rows], va(R_POST_MIX))
        h1_parts.append(h1)
        hn1_parts.append(_rms(h1, va(R_PRE_FFN)).astype(bf16))
    out['h1'] = jnp.concatenate(h1_parts, axis=0)
    out['hn1'] = jnp.concatenate(hn1_parts, axis=0)


def _ple_finish(h_out_ref, h_ref, e_ref, gate_ref, va, done=None):
    for r in range(0, h_ref.shape[0], ROW_PIECE):
        yield 'v', 300
        rows = slice(r, r + ROW_PIECE)
        res = h_ref[rows] + _rms(e_ref[rows] * _sigmoid(gate_ref[rows]), va(R_PLE))
        h_out_ref[rows] = res
        if done is not None:
            done.append(_done_token(res))


def _ffn_ple_steps(h_out_ref, h1_ref, hn1_ref, p_ref, va, w_gate_ref, w_up_ref, w_down_ref, w_pg_ref,
                   w_pp_ref, scratch, *, deferred, finish_done=None):
    act_scr, f_scr, gate_scr, h2_scr, e_scr = scratch
    k = max(1, h1_ref.shape[0] // COST_ROWS)
    if deferred:
        yield from _ple_finish(h_out_ref, h2_scr, e_scr, gate_scr, va, finish_done)
    nout = D_MODEL // MXU_TILE
    for bi, (lo, hi) in enumerate(FF_BLOCKS):
        for c0 in range(lo, hi, MXU_TILE):
            yield 'x', X_PIECE * k
            gt = _dot(hn1_ref[...], w_gate_ref[:, c0:c0 + MXU_TILE])
            yield 'x', X_PIECE * k
            up = _dot(hn1_ref[...], w_up_ref[:, c0:c0 + MXU_TILE])
            yield 'v', 130 * k
            act_scr[:, c0 - lo:c0 - lo + MXU_TILE] = (gt * _sigmoid(gt) * up).astype(bf16)
        for j in range(nout):
            yield 'x', (hi - lo) // 4 * k
            cols = slice(j * MXU_TILE, (j + 1) * MXU_TILE)
            part = _dot(act_scr[:, 0:hi - lo], w_down_ref[lo:hi, cols])
            f_scr[:, cols] = part if bi == 0 else f_scr[:, cols] + part
    hb = []
    for r in range(0, h1_ref.shape[0], ROW_PIECE):
        yield 'v', 150
        rows = slice(r, r + ROW_PIECE)
        h = h1_ref[rows] + _rms(f_scr[rows], va(R_POST_FFN))
        h2_scr[rows] = h
        hb.append(h.astype(bf16))
    hb = jnp.concatenate(hb, axis=0)
    for j in range(nout):
        yield 'x', X_PIECE * k, LATE
        gate_scr[:, j * MXU_TILE:(j + 1) * MXU_TILE] = _dot(hb, w_pg_ref[:, j * MXU_TILE:(j + 1) * MXU_TILE])
    yield 'x', X_PIECE * k, LATE
    e_scr[...] = _dot(p_ref[...].astype(bf16), w_pp_ref[...])
    if not deferred:
        yield from _ple_finish(h_out_ref, h2_scr, e_scr, gate_scr, va)


def _merge(lead, fill=None):
    spent = {'x': 0.0, 'v': 0.0}

    def advance(stream, head):
        spent[head[0]] += head[1]
        return next(stream, None)

    lead_head = next(lead, None)
    fill_head = next(fill, None) if fill is not None else None
    while lead_head is not None:
        while (fill_head is not None and (LATE in lead_head or LATE not in fill_head)
               and spent['x'] <= MATRIX_TO_VECTOR_TIME * spent['v']):
            fill_head = advance(fill, fill_head)
        lead_head = advance(lead, lead_head)
    while fill_head is not None:
        fill_head = advance(fill, fill_head)


def _row_getters(va_ref, vb_ref):
    return (lambda r: va_ref[r:r + 1, :]), (lambda r: vb_ref[r:r + 1, :])


def _prompt_kernel(h_ref, h_next_ref, p_ref, va_ref, vb_ref, w_in_ref, w_u_ref, w_dt_ref, pool_w_ref,
                   w_out_ref, w_gate_ref, w_up_ref, w_down_ref, w_pg_ref, w_pp_ref,
                   h_out_ref, ssm_out_ref, conv_out_ref, pool_out_ref,
                   xbc_ext, pool_ext, st_ref, y_scr, h1_scr, hn1_scr, act_scr, f_scr, gate_scr, h2_scr, e_scr,
                   hn_scr, *, tiles_per_seq, tile):
    n = pl.program_id(0)
    t = n % tiles_per_seq
    va, vb = _row_getters(va_ref, vb_ref)

    @pl.when(n == 0)
    def _():
        for ref in (h1_scr, hn1_scr, f_scr, gate_scr, h2_scr, e_scr):
            ref[...] = jnp.zeros(ref.shape, ref.dtype)
        hn_scr[...] = _rms(h_ref[...], va(R_PRE_MIX)).astype(bf16)

    @pl.when(t == 0)
    def _():
        xbc_ext[:, 0:HDR, :] = jnp.zeros((1, HDR, CONV_DIM), f32)
        pool_ext[:, 0:HDR, :] = jnp.zeros((1, HDR, D_POOL), f32)
        st_ref[...] = jnp.zeros(st_ref.shape, f32)

    out = {'finish_done': []}
    mixer = _mixer_steps(out, h_ref, (None, None, None), va, vb, (w_in_ref, w_u_ref, w_dt_ref), pool_w_ref,
                         w_out_ref,
                         (None, conv_out_ref, pool_out_ref), (xbc_ext, pool_ext, st_ref, y_scr),
                         nseg=1, seg_len=tile, t=t, prenorm=(hn_scr, h_next_ref))
    ffn = _ffn_ple_steps(h_out_ref, h1_scr, hn1_scr, p_ref, va, w_gate_ref, w_up_ref, w_down_ref, w_pg_ref,
                         w_pp_ref, (act_scr, f_scr, gate_scr, h2_scr, e_scr), deferred=True,
                         finish_done=out['finish_done'])
    _merge(mixer, ffn)
    h1_scr[...] = out['h1']
    hn1_scr[...] = out['hn1']

    @pl.when(t == tiles_per_seq - 1)
    def _():
        for g in range(SSM_GROUPS):
            ssm_out_ref[g * HPG:(g + 1) * HPG] = st_ref[g].T.reshape(HPG, SSM_HEAD_DIM, D_STATE)


def _sample_kernel(h_ref, p_ref, ssm0_ref, conv0_ref, pool0_ref, va_ref, vb_ref, w_in_ref, w_u_ref, w_dt_ref, pool_w_ref,
                   w_out_ref, w_gate_ref, w_up_ref, w_down_ref, w_pg_ref, w_pp_ref,
                   h_out_ref, ssm_out_ref, conv_out_ref, pool_out_ref,
                   xbc_ext, pool_ext, st_ref, y_scr, h1_scr, hn1_scr, act_scr, f_scr, gate_scr, h2_scr, e_scr,
                   *, nseg, seg_len):
    va, vb = _row_getters(va_ref, vb_ref)
    out = {}
    _merge(_mixer_steps(out, h_ref, (ssm0_ref, conv0_ref, pool0_ref), va, vb, (w_in_ref, w_u_ref, w_dt_ref), pool_w_ref,
                        w_out_ref, (ssm_out_ref, conv_out_ref, pool_out_ref),
                        (xbc_ext, pool_ext, st_ref, y_scr), nseg=nseg, seg_len=seg_len, t=None))
    h1_scr[...] = out['h1']
    hn1_scr[...] = out['hn1']
    _merge(_ffn_ple_steps(h_out_ref, h1_scr, hn1_scr, p_ref, va, w_gate_ref, w_up_ref, w_down_ref, w_pg_ref,
                          w_pp_ref, (act_scr, f_scr, gate_scr, h2_scr, e_scr), deferred=False))


def _resident(shape_tail, layer):
    nd = len(shape_tail)
    return pl.BlockSpec((None,) + tuple(shape_tail), lambda *_: (layer,) + (0,) * nd,
                        pipeline_mode=pl.Buffered(1))


def _weight_specs(layer):
    return [
        _resident((16, D_MODEL), layer),
        _resident((8, D_SSM), layer),
        _resident((D_MODEL, ZX_COLS), layer),
        _resident((D_MODEL, D_POOL), layer),
        _resident((D_MODEL, D_SSM), layer),
        _resident((2, MXU_TILE, MXU_TILE), layer),
        _resident((D_MODEL, D_MODEL), layer),
        _resident((D_MODEL, D_FF), layer),
        _resident((D_MODEL, D_FF), layer),
        _resident((D_FF, D_MODEL), layer),
        _resident((D_MODEL, D_MODEL), layer),
        _resident((D_PLE, D_MODEL), layer),
    ]


def _scratch(nseg, seg_len, pipelined):
    m = nseg * seg_len
    pre = [pltpu.VMEM((m, D_MODEL), bf16)] if pipelined else []
    return [
        pltpu.VMEM((nseg, HDR + seg_len, CONV_DIM), f32),
        pltpu.VMEM((nseg, HDR + seg_len, D_POOL), f32),
        pltpu.VMEM((SSM_GROUPS, D_STATE, GROUP_W), f32),
        pltpu.VMEM((m, D_SSM), f32),
        pltpu.VMEM((m, D_MODEL), f32),
        pltpu.VMEM((m, D_MODEL), bf16),
        pltpu.VMEM((m, max(hi - lo for lo, hi in FF_BLOCKS)), bf16),
        pltpu.VMEM((m, D_MODEL), f32),
        pltpu.VMEM((m, D_MODEL), f32),
        pltpu.VMEM((m, D_MODEL), f32),
        pltpu.VMEM((m, D_MODEL), f32),
    ] + pre


def _prompt_layer(layer, h, p_all, weights):
    nb, nl, _ = h.shape
    T = PROMPT_TILE
    assert nl % T == 0 and T % Q == 0
    nt = nl // T
    ntiles = nb * nt

    def tile_at(offset):
        def index(n):
            m = jnp.clip(n + offset, 0, ntiles - 1)
            return m // nt, m % nt
        return index

    nxt, cur, prev, prev2 = tile_at(1), tile_at(0), tile_at(-1), tile_at(-2)

    in_specs = [
        pl.BlockSpec((None, T, D_MODEL), lambda n: (*cur(n), 0)),
        pl.BlockSpec((None, T, D_MODEL), lambda n: (*nxt(n), 0)),
        pl.BlockSpec((None, None, T, D_PLE), lambda n: (layer, *prev(n), 0)),
    ] + _weight_specs(layer)
    out_shape = (
        jax.ShapeDtypeStruct((nb, nl, D_MODEL), f32),
        jax.ShapeDtypeStruct((nb, SSM_HEADS, SSM_HEAD_DIM, D_STATE), f32),
        jax.ShapeDtypeStruct((nb, 1, CONV_W - 1, CONV_DIM), f32),
        jax.ShapeDtypeStruct((nb, 1, POOL_HIST, D_POOL), f32),
    )
    out_specs = (
        pl.BlockSpec((None, T, D_MODEL), lambda n: (*prev2(n), 0)),
        pl.BlockSpec((None, SSM_HEADS, SSM_HEAD_DIM, D_STATE), lambda n: (cur(n)[0], 0, 0, 0)),
        pl.BlockSpec((None, 1, CONV_W - 1, CONV_DIM), lambda n: (cur(n)[0], 0, 0, 0)),
        pl.BlockSpec((None, 1, POOL_HIST, D_POOL), lambda n: (cur(n)[0], 0, 0, 0)),
    )
    h_new, ssm, conv, pool = pl.pallas_call(
        functools.partial(_prompt_kernel, tiles_per_seq=nt, tile=T),
        grid=(ntiles + 2,), in_specs=in_specs, out_specs=out_specs, out_shape=out_shape,
        scratch_shapes=_scratch(1, T, True),
        compiler_params=pltpu.CompilerParams(
            dimension_semantics=("arbitrary",), vmem_limit_bytes=VMEM_LIMIT),
        name=f"prompt_layer{layer}",
    )(h, h, p_all, *weights)
    return h_new, ssm, conv[:, 0], pool[:, 0]


def _sample_layer(layer, h2d, p_all, ssm_all, conv_all, pool_all, weights, nb, nl):
    assert nl % Q == 0
    M = nb * nl
    full = lambda *tail: pl.BlockSpec((None,) + tail, lambda i: (layer,) + (0,) * len(tail))
    in_specs = [
        pl.BlockSpec((M, D_MODEL), lambda i: (0, 0)),
        full(M, D_PLE),
        full(nb, SSM_HEADS, SSM_HEAD_DIM, D_STATE),
        full(nb, CONV_W - 1, CONV_DIM),
        full(nb, POOL_HIST, D_POOL),
    ] + _weight_specs(layer)
    out_shape = (
        jax.ShapeDtypeStruct((M, D_MODEL), f32),
        jax.ShapeDtypeStruct((nb, SSM_HEADS, SSM_HEAD_DIM, D_STATE), f32),
        jax.ShapeDtypeStruct((nb, CONV_W - 1, CONV_DIM), f32),
        jax.ShapeDtypeStruct((nb, POOL_HIST, D_POOL), f32),
    )
    out_specs = tuple(pl.BlockSpec(s.shape, lambda i, n=len(s.shape): (0,) * n) for s in out_shape)
    return pl.pallas_call(
        functools.partial(_sample_kernel, nseg=nb, seg_len=nl),
        grid=(1,), in_specs=in_specs, out_specs=out_specs, out_shape=out_shape,
        scratch_shapes=_scratch(nb, nl, False),
        compiler_params=pltpu.CompilerParams(
            dimension_semantics=("arbitrary",), vmem_limit_bytes=VMEM_LIMIT),
        name=f"sample_layer{layer}",
    )(h2d, p_all, ssm_all, conv_all, pool_all, *weights)


def _pack_weights(pre_mix_g, w_in, conv_w, conv_b, dt_bias, a_log, d_skip, ssm_norm_g, pool_w, pool_b,
                  pool_scale, w_out, post_mix_g, pre_ffn_g, w_gate, w_up, w_down, post_ffn_g, w_ple_gate,
                  w_ple_proj, ple_norm_g):
    dt_cols = D_SSM + CONV_DIM
    w_in = w_in.astype(bf16)
    assert dt_cols == ZX_COLS
    w_u = w_in[:, :, dt_cols + SSM_HEADS:]
    w_dt = jnp.repeat(w_in[:, :, dt_cols:dt_cols + SSM_HEADS], SSM_HEAD_DIM, axis=2)
    rep = lambda v: jnp.repeat(v, SSM_HEAD_DIM, axis=1)
    va = jnp.stack([pre_mix_g, post_mix_g, pre_ffn_g, post_ffn_g, ple_norm_g, conv_b]
                   + [conv_w[:, k] for k in range(CONV_W)], axis=1)
    va = jnp.pad(va, ((0, 0), (0, 16 - va.shape[1]), (0, 0)))
    vb = jnp.stack([rep(dt_bias), rep(a_log), rep(d_skip), ssm_norm_g,
                    pool_b.reshape(DEPTH, D_POOL), pool_scale], axis=1)
    vb = jnp.pad(vb, ((0, 0), (0, 8 - vb.shape[1]), (0, 0)))
    zeros = jnp.zeros((DEPTH, POOL_GROUP, POOL_GROUP), f32)
    pw = [jnp.concatenate([jnp.concatenate([pool_w[:, 2 * k], zeros], axis=2),
                           jnp.concatenate([zeros, pool_w[:, 2 * k + 1]], axis=2)], axis=1)
          for k in range(2)]
    pool_w2 = jnp.stack(pw, axis=1).astype(bf16)
    return (va, vb, w_in, w_u, w_dt, pool_w2, w_out.astype(bf16), w_gate.astype(bf16), w_up.astype(bf16),
            w_down.astype(bf16), w_ple_gate.astype(bf16), w_ple_proj.astype(bf16))


def kernel(x_prompt, x_sample, state_ssm, state_conv, state_pool, p_prompt, p_sample, pre_mix_g, w_in, conv_w, conv_b, dt_bias, a_log, d_skip, ssm_norm_g, pool_w, pool_b, pool_scale, w_out, post_mix_g, pre_ffn_g, w_gate, w_up, w_down, post_ffn_g, w_ple_gate, w_ple_proj, ple_norm_g):
    weights = _pack_weights(pre_mix_g, w_in, conv_w, conv_b, dt_bias, a_log, d_skip, ssm_norm_g, pool_w,
                            pool_b, pool_scale, w_out, post_mix_g, pre_ffn_g, w_gate, w_up, w_down,
                            post_ffn_g, w_ple_gate, w_ple_proj, ple_norm_g)
    nbs, nls, _ = x_sample.shape
    hp = x_prompt
    hs = x_sample.reshape(nbs * nls, D_MODEL)
    p_sample2d = p_sample.reshape(DEPTH, nbs * nls, D_PLE)
    outs_p, outs_s = [], []
    for i in range(DEPTH):
        hp, ssm, conv, pool = _prompt_layer(i, hp, p_prompt, weights)
        outs_p.append((ssm, conv, pool))
        hs, ssm, conv, pool = _sample_layer(i, hs, p_sample2d, state_ssm, state_conv, state_pool,
                                            weights, nbs, nls)
        outs_s.append((ssm, conv, pool))
    stack = lambda outs, j: jnp.stack([o[j] for o in outs])
    return (hp, hs.reshape(nbs, nls, D_MODEL),
            stack(outs_p, 0), stack(outs_p, 1), stack(outs_p, 2),
            stack(outs_s, 0), stack(outs_s, 1), stack(outs_s, 2))
```

```python
import functools

import jax
import jax.numpy as jnp
from jax import lax
from jax.experimental import pallas as pl
from jax.experimental.pallas import tpu as pltpu

D_MODEL = 1024
DEPTH = 4
PAST_LEN = 4096
D_SSM = 512
D_POOL = 512
SSM_HEAD_DIM = 64
SSM_HEADS = 8
SSM_GROUPS = 2
HPG = SSM_HEADS // SSM_GROUPS
D_STATE = 128
CONV_W = 4
CONV_DIM = 1024
POOL_WINDOWS = (2, 4, 8, 16)
POOL_GROUP = 128
POOL_HIST = 15
D_FF = 2816
D_PLE = 256
EPS = 1e-6

Q = 64
GROUP_W = D_SSM // SSM_GROUPS
HDR = 16
PROMPT_TILE = 256
FF_BLOCKS = ((0, 1024), (1024, 2048), (2048, D_FF))
NEG = -1e30
LANES, SUBLANES = 128, 8
MXU_TILE = 256
COST_ROWS = 256
X_PIECE = 256
LATE = 'late'
ROW_PIECE = 128
MATRIX_TO_VECTOR_TIME = 1.6
VMEM_CAPACITY_V7X = 64 * 1024 * 1024
VMEM_LIMIT = VMEM_CAPACITY_V7X - 8 * 1024 * 1024

Z0, XBC0, U0, DT0, IN_EXT = 0, 512, 1536, 2048, 2560

(R_PRE_MIX, R_POST_MIX, R_PRE_FFN, R_POST_FFN, R_PLE, R_CONV_B, R_CONV_W0) = range(7)
(R_DT_BIAS, R_A_LOG, R_D_SKIP, R_SSM_G, R_POOL_B, R_POOL_SCALE) = range(6)

f32 = jnp.float32
bf16 = jnp.bfloat16


def _rms(x, g_row):
    ms = jnp.mean(x * x, axis=-1, keepdims=True)
    return x * lax.rsqrt(ms + EPS) * g_row


def _sigmoid(x):
    return 1.0 / (1.0 + jnp.exp(-x))


def _dot(a, b):
    return jnp.dot(a, b, preferred_element_type=f32)


def _done_token(x):
    bits = pltpu.bitcast(x, jnp.uint32)
    acc = bits[:, 0:LANES]
    for l in range(LANES, x.shape[1], LANES):
        acc = acc | bits[:, l:l + LANES]
    rows = acc[0:SUBLANES]
    for r in range(SUBLANES, x.shape[0], SUBLANES):
        rows = rows | acc[r:r + SUBLANES]
    zero = lax.shift_right_logical(lax.shift_right_logical(rows, jnp.uint32(16)), jnp.uint32(16))
    return zero[0:1].astype(f32)


def _take_tokens(tokens, row):
    while tokens:
        row = row + jnp.concatenate([tokens.pop()] * (row.shape[1] // LANES), axis=1)
    return row


def _chunk_cumsum(x):
    row = lax.broadcasted_iota(jnp.int32, x.shape, 0) % Q
    sh = 1
    while sh < Q:
        x = x + jnp.where(row >= sh, pltpu.roll(x, sh, axis=0), 0.0)
        sh *= 2
    return x


def _ssd_block(xg, dtg, ag, bg, cg, s, masks):
    eye2, causal2, blockmask = masks
    a_last = ag[Q - 1:Q, :]
    xdt = xg * dtg
    bb = bg.astype(bf16)
    cb_ = cg.astype(bf16)
    b2 = jnp.concatenate([bb, bb], axis=0)
    cb2 = lax.dot_general(cb_, b2, (((1,), (1,)), ((), ())),
                          preferred_element_type=f32)
    ys = []
    for q in range(2):
        p_col = ag[:, q * LANES:(q + 1) * LANES]
        p_row = jnp.sum(jnp.where(eye2, p_col, 0.0), axis=0, keepdims=True)
        lmat = jnp.exp(jnp.where(causal2, p_col - p_row, NEG))
        g = (cb2 * lmat).astype(bf16)
        xp = xdt[:, q * LANES:(q + 1) * LANES]
        bd = jnp.where(blockmask, jnp.concatenate([xp, xp], axis=0), 0.0).astype(bf16)
        ys.append(_dot(g, bd))
    y_diag = jnp.concatenate(ys, axis=1)
    y_off = _dot(cb_, s.astype(bf16)) * jnp.exp(ag)
    xdtd = (xdt * jnp.exp(a_last - ag)).astype(bf16)
    states = _dot(bg.T.astype(bf16), xdtd)
    s_new = s * jnp.exp(a_last) + states
    return y_diag + y_off, s_new


def _mixer_steps(out, h_ref, state_in, va, vb, w_in_ref, pool_w_ref, w_out_ref, state_out, scratch,
                 *, nseg, seg_len, t, prenorm=None):
    streaming = t is not None
    ssm0_ref, conv0_ref, pool0_ref = state_in
    ssm_out_ref, conv_out_ref, pool_out_ref = state_out
    xbc_ext, pool_ext, st_ref, y_scr = scratch
    L = seg_len
    k = max(1, nseg * L // COST_ROWS)
    W = GROUP_W
    seg_rows = [slice(s * L, (s + 1) * L) for s in range(nseg)]
    row_pieces = [slice(r, r + ROW_PIECE) for r in range(0, nseg * L, ROW_PIECE)]
    H = W // 2
    prenorm_done = []

    if prenorm is None:
        yield 'v', 320 * k
        hn_val = _rms(h_ref[...], va(R_PRE_MIX)).astype(bf16)
        hn = lambda: hn_val
    else:
        hn = lambda: prenorm[0][...]
    if not streaming:
        for s in range(nseg):
            xbc_ext[s, HDR - (CONV_W - 1):HDR, :] = conv0_ref[s]
            pool_ext[s, 0:HDR - POOL_HIST, :] = jnp.zeros((HDR - POOL_HIST, D_POOL), f32)
            pool_ext[s, HDR - POOL_HIST:HDR, :] = pool0_ref[s]

    dts = []
    for g in range(SSM_GROUPS):
        yield 'x', X_PIECE * k
        dts.append(_dot(hn(), w_in_ref[:, DT0 + g * W:DT0 + (g + 1) * W]))
    for j in range(CONV_DIM // W):
        yield 'x', X_PIECE * k
        piece = _dot(hn(), w_in_ref[:, XBC0 + j * W:XBC0 + (j + 1) * W])
        for s in range(nseg):
            xbc_ext[s, HDR:HDR + L, j * W:(j + 1) * W] = piece[seg_rows[s]]
    for s in range(nseg):
        conv_out_ref[s] = xbc_ext[s, HDR + L - (CONV_W - 1):HDR + L, :]

    acums = []
    for g in range(SSM_GROUPS):
        dt_halves, acum_halves = [], []
        for l0 in range(g * W, (g + 1) * W, H):
            yield 'v', 170 * k
            bias = _take_tokens(out.setdefault('finish_done', []), vb(R_DT_BIAS)[:, l0:l0 + H])
            d = dts[g][:, l0 - g * W:l0 - g * W + H] + bias
            d = jnp.maximum(d, 0.0) + jnp.log(1.0 + jnp.exp(-jnp.abs(d)))
            dt_halves.append(d)
            acum_halves.append(_chunk_cumsum(d * -jnp.exp(vb(R_A_LOG)[:, l0:l0 + H])))
        dts[g] = jnp.concatenate(dt_halves, axis=1)
        acums.append(jnp.concatenate(acum_halves, axis=1))

    xbc = []
    for j in range(CONV_DIM // H):
        yield 'v', 200 * k
        ls = slice(j * H, (j + 1) * H)
        parts = []
        for s in range(nseg):
            acc = va(R_CONV_B)[:, ls]
            for kk in range(CONV_W):
                lo = HDR - (CONV_W - 1) + kk
                acc = acc + va(R_CONV_W0 + kk)[:, ls] * xbc_ext[s, lo:lo + L, ls]
            parts.append(acc)
        piece = parts[0] if nseg == 1 else jnp.concatenate(parts, axis=0)
        xbc.append(piece * _sigmoid(piece))
    xbc = [jnp.concatenate(xbc[j:j + W // H], axis=1) for j in range(0, len(xbc), W // H)]
    xs, bmat, cmat = xbc[0:SSM_GROUPS], xbc[SSM_GROUPS], xbc[SSM_GROUPS + 1]
    if streaming:
        xbc_ext[:, 0:HDR, :] = xbc_ext[:, L:L + HDR, :]

    zs = []
    for g in range(SSM_GROUPS):
        yield 'x', X_PIECE * k
        zs.append(_dot(hn(), w_in_ref[:, Z0 + g * W:Z0 + (g + 1) * W]))
    for j in range(D_POOL // W):
        yield 'x', X_PIECE * k
        piece = _dot(hn(), w_in_ref[:, U0 + j * W:U0 + (j + 1) * W])
        for s in range(nseg):
            pool_ext[s, HDR:HDR + L, j * W:(j + 1) * W] = piece[seg_rows[s]]
    for s in range(nseg):
        pool_out_ref[s] = pool_ext[s, L + HDR - POOL_HIST:L + HDR, :]
    if prenorm is not None:
        for rows in row_pieces:
            yield 'v', 160
            hn_next = _rms(prenorm[1][rows], va(R_PRE_MIX))
            prenorm[0][rows] = hn_next.astype(bf16)
            prenorm_done.append(_done_token(hn_next))

    assert 2 * SSM_HEAD_DIM == LANES and Q == SSM_HEAD_DIM
    lane = lax.broadcasted_iota(jnp.int32, (Q, LANES), 1)
    rowq = lax.broadcasted_iota(jnp.int32, (Q, LANES), 0)
    eye2 = rowq == (lane % Q)
    causal2 = rowq >= (lane % Q)
    row2 = lax.broadcasted_iota(jnp.int32, (2 * Q, LANES), 0)
    lane2 = lax.broadcasted_iota(jnp.int32, (2 * Q, LANES), 1)
    blockmask = (row2 // Q) == (lane2 // SSM_HEAD_DIM)
    masks = (eye2, causal2, blockmask)

    for s in range(nseg):
        states = []
        for g in range(SSM_GROUPS):
            if streaming:
                states.append(st_ref[g])
            else:
                states.append(ssm0_ref[s, g * HPG:(g + 1) * HPG].reshape(W, D_STATE).T)
        for c in range(L // Q):
            rs = slice(s * L + c * Q, s * L + (c + 1) * Q)
            for g in range(SSM_GROUPS):
                yield 'v', 200
                bg = bmat[rs, g * D_STATE:(g + 1) * D_STATE]
                cg = cmat[rs, g * D_STATE:(g + 1) * D_STATE]
                yg, states[g] = _ssd_block(xs[g][rs], dts[g][rs], acums[g][rs], bg, cg, states[g], masks)
                y_scr[rs, g * W:(g + 1) * W] = yg
        for g in range(SSM_GROUPS):
            if streaming:
                st_ref[g] = states[g]
            else:
                ssm_out_ref[s, g * HPG:(g + 1) * HPG] = states[g].T.reshape(HPG, SSM_HEAD_DIM, D_STATE)

    ys = []
    for g in range(SSM_GROUPS):
        gl = slice(g * W, (g + 1) * W)
        parts = []
        for rows in row_pieces:
            yield 'v', 130
            d_skip = _take_tokens(prenorm_done, vb(R_D_SKIP)[:, gl])
            yg = y_scr[rows, gl] + d_skip * xs[g][rows]
            yg = yg * (zs[g][rows] * _sigmoid(zs[g][rows]))
            yg = yg * lax.rsqrt(jnp.mean(yg * yg, axis=-1, keepdims=True) + EPS)
            parts.append((yg * vb(R_SSM_G)[:, gl]).astype(bf16))
        ys.append(jnp.concatenate(parts, axis=0))

    pos0 = t * L if streaming else PAST_LEN
    pos1 = pos0 + 1 + lax.broadcasted_iota(jnp.int32, (L, POOL_GROUP), 0)
    assert POOL_WINDOWS == tuple(2 ** (i + 1) for i in range(len(POOL_WINDOWS))) and HDR > POOL_HIST
    pooled = []
    sums = None
    for gi, w in enumerate(POOL_WINDOWS):
        yield 'v', (150 - 30 * gi) * k
        if sums is None:
            sums = [pool_ext[s] for s in range(nseg)]
        sums = [x + pltpu.roll(x, w // 2, axis=0) for x in sums]
        cnt = jnp.minimum(pos1, w).astype(f32)
        ls = slice(gi * POOL_GROUP, (gi + 1) * POOL_GROUP)
        parts = [x[HDR:HDR + L, 0:POOL_GROUP] / cnt - pool_ext[s, HDR:HDR + L, ls]
                 for s, x in enumerate(sums)]
        piece = parts[0] if nseg == 1 else jnp.concatenate(parts, axis=0)
        pooled.append(piece.astype(bf16))
        sums = [x[:, POOL_GROUP:] for x in sums]
    if streaming:
        pool_ext[:, 0:HDR, :] = pool_ext[:, L:L + HDR, :]

    yield 'x', X_PIECE // 2 * k
    pool_o = jnp.concatenate([_dot(jnp.concatenate(pooled[0:2], axis=1), pool_w_ref[0]),
                              _dot(jnp.concatenate(pooled[2:4], axis=1), pool_w_ref[1])], axis=1)
    pool_o = ((pool_o + vb(R_POOL_B)) * vb(R_POOL_SCALE)).astype(bf16)
    mixer_out = jnp.concatenate(ys + [pool_o], axis=1)
    mix = []
    for j in range(D_MODEL // MXU_TILE):
        yield 'x', X_PIECE * k
        mix.append(_dot(mixer_out, w_out_ref[:, j * MXU_TILE:(j + 1) * MXU_TILE]))
    mix = jnp.concatenate(mix, axis=1)
    h1_parts, hn1_parts = [], []
    for rows in row_pieces:
        yield 'v', 300, LATE
        h1 = h_ref[rows] + _rms(mix[rows], va(R_POST_MIX))
        h1_parts.append(h1)
        hn1_parts.append(_rms(h1, va(R_PRE_FFN)).astype(bf16))
    out['h1'] = jnp.concatenate(h1_parts, axis=0)
    out['hn1'] = jnp.concatenate(hn1_parts, axis=0)


def _ple_finish(h_out_ref, h_ref, e_ref, gate_ref, va, done=None):
    for r in range(0, h_ref.shape[0], ROW_PIECE):
        yield 'v', 300
        rows = slice(r, r + ROW_PIECE)
        res = h_ref[rows] + _rms(e_ref[rows] * _sigmoid(gate_ref[rows]), va(R_PLE))
        h_out_ref[rows] = res
        if done is not None:
            done.append(_done_token(res))


def _ffn_ple_steps(h_out_ref, h1_ref, hn1_ref, p_ref, va, w_gate_ref, w_up_ref, w_down_ref, w_pg_ref,
                   w_pp_ref, scratch, *, deferred, finish_done=None):
    act_scr, f_scr, gate_scr, h2_scr, e_scr = scratch
    k = max(1, h1_ref.shape[0] // COST_ROWS)
    if deferred:
        yield from _ple_finish(h_out_ref, h2_scr, e_scr, gate_scr, va, finish_done)
    nout = D_MODEL // MXU_TILE
    for bi, (lo, hi) in enumerate(FF_BLOCKS):
        for c0 in range(lo, hi, MXU_TILE):
            yield 'x', X_PIECE * k
            gt = _dot(hn1_ref[...], w_gate_ref[:, c0:c0 + MXU_TILE])
            yield 'x', X_PIECE * k
            up = _dot(hn1_ref[...], w_up_ref[:, c0:c0 + MXU_TILE])
            yield 'v', 130 * k
            act_scr[:, c0 - lo:c0 - lo + MXU_TILE] = (gt * _sigmoid(gt) * up).astype(bf16)
        for j in range(nout):
            yield 'x', (hi - lo) // 4 * k
            cols = slice(j * MXU_TILE, (j + 1) * MXU_TILE)
            part = _dot(act_scr[:, 0:hi - lo], w_down_ref[lo:hi, cols])
            f_scr[:, cols] = part if bi == 0 else f_scr[:, cols] + part
    hb = []
    for r in range(0, h1_ref.shape[0], ROW_PIECE):
        yield 'v', 150
        rows = slice(r, r + ROW_PIECE)
        h = h1_ref[rows] + _rms(f_scr[rows], va(R_POST_FFN))
        h2_scr[rows] = h
        hb.append(h.astype(bf16))
    hb = jnp.concatenate(hb, axis=0)
    for j in range(nout):
        yield 'x', X_PIECE * k, LATE
        gate_scr[:, j * MXU_TILE:(j + 1) * MXU_TILE] = _dot(hb, w_pg_ref[:, j * MXU_TILE:(j + 1) * MXU_TILE])
    yield 'x', X_PIECE * k, LATE
    e_scr[...] = _dot(p_ref[...].astype(bf16), w_pp_ref[...])
    if not deferred:
        yield from _ple_finish(h_out_ref, h2_scr, e_scr, gate_scr, va)


def _merge(lead, fill=None):
    spent = {'x': 0.0, 'v': 0.0}

    def advance(stream, head):
        spent[head[0]] += head[1]
        return next(stream, None)

    lead_head = next(lead, None)
    fill_head = next(fill, None) if fill is not None else None
    while lead_head is not None:
        while (fill_head is not None and (LATE in lead_head or LATE not in fill_head)
               and spent['x'] <= MATRIX_TO_VECTOR_TIME * spent['v']):
            fill_head = advance(fill, fill_head)
        lead_head = advance(lead, lead_head)
    while fill_head is not None:
        fill_head = advance(fill, fill_head)


def _row_getters(va_ref, vb_ref):
    return (lambda r: va_ref[r:r + 1, :]), (lambda r: vb_ref[r:r + 1, :])


def _prompt_kernel(h_ref, h_next_ref, p_ref, va_ref, vb_ref, w_in_ref, pool_w_ref, w_out_ref, w_gate_ref,
                   w_up_ref, w_down_ref, w_pg_ref, w_pp_ref,
                   h_out_ref, ssm_out_ref, conv_out_ref, pool_out_ref,
                   xbc_ext, pool_ext, st_ref, y_scr, h1_scr, hn1_scr, act_scr, f_scr, gate_scr, h2_scr, e_scr,
                   hn_scr, *, tiles_per_seq, tile):
    n = pl.program_id(0)
    t = n % tiles_per_seq
    va, vb = _row_getters(va_ref, vb_ref)

    @pl.when(n == 0)
    def _():
        for ref in (h1_scr, hn1_scr, f_scr, gate_scr, h2_scr, e_scr):
            ref[...] = jnp.zeros(ref.shape, ref.dtype)
        hn_scr[...] = _rms(h_ref[...], va(R_PRE_MIX)).astype(bf16)

    @pl.when(t == 0)
    def _():
        xbc_ext[:, 0:HDR, :] = jnp.zeros((1, HDR, CONV_DIM), f32)
        pool_ext[:, 0:HDR, :] = jnp.zeros((1, HDR, D_POOL), f32)
        st_ref[...] = jnp.zeros(st_ref.shape, f32)

    out = {'finish_done': []}
    mixer = _mixer_steps(out, h_ref, (None, None, None), va, vb, w_in_ref, pool_w_ref, w_out_ref,
                         (None, conv_out_ref, pool_out_ref), (xbc_ext, pool_ext, st_ref, y_scr),
                         nseg=1, seg_len=tile, t=t, prenorm=None)
    ffn = _ffn_ple_steps(h_out_ref, h1_scr, hn1_scr, p_ref, va, w_gate_ref, w_up_ref, w_down_ref, w_pg_ref,
                         w_pp_ref, (act_scr, f_scr, gate_scr, h2_scr, e_scr), deferred=True,
                         finish_done=out['finish_done'])
    _merge(mixer, ffn)
    h1_scr[...] = out['h1']
    hn1_scr[...] = out['hn1']

    @pl.when(t == tiles_per_seq - 1)
    def _():
        for g in range(SSM_GROUPS):
            ssm_out_ref[g * HPG:(g + 1) * HPG] = st_ref[g].T.reshape(HPG, SSM_HEAD_DIM, D_STATE)


def _sample_kernel(h_ref, p_ref, ssm0_ref, conv0_ref, pool0_ref, va_ref, vb_ref, w_in_ref, pool_w_ref,
                   w_out_ref, w_gate_ref, w_up_ref, w_down_ref, w_pg_ref, w_pp_ref,
                   h_out_ref, ssm_out_ref, conv_out_ref, pool_out_ref,
                   xbc_ext, pool_ext, st_ref, y_scr, h1_scr, hn1_scr, act_scr, f_scr, gate_scr, h2_scr, e_scr,
                   *, nseg, seg_len):
    va, vb = _row_getters(va_ref, vb_ref)
    out = {}
    _merge(_mixer_steps(out, h_ref, (ssm0_ref, conv0_ref, pool0_ref), va, vb, w_in_ref, pool_w_ref,
                        w_out_ref, (ssm_out_ref, conv_out_ref, pool_out_ref),
                        (xbc_ext, pool_ext, st_ref, y_scr), nseg=nseg, seg_len=seg_len, t=None))
    h1_scr[...] = out['h1']
    hn1_scr[...] = out['hn1']
    _merge(_ffn_ple_steps(h_out_ref, h1_scr, hn1_scr, p_ref, va, w_gate_ref, w_up_ref, w_down_ref, w_pg_ref,
                          w_pp_ref, (act_scr, f_scr, gate_scr, h2_scr, e_scr), deferred=False))


def _resident(shape_tail, layer):
    nd = len(shape_tail)
    return pl.BlockSpec((None,) + tuple(shape_tail), lambda *_: (layer,) + (0,) * nd,
                        pipeline_mode=pl.Buffered(1))


def _weight_specs(layer):
    return [
        _resident((16, D_MODEL), layer),
        _resident((8, D_SSM), layer),
        _resident((D_MODEL, IN_EXT), layer),
        _resident((2, MXU_TILE, MXU_TILE), layer),
        _resident((D_MODEL, D_MODEL), layer),
        _resident((D_MODEL, D_FF), layer),
        _resident((D_MODEL, D_FF), layer),
        _resident((D_FF, D_MODEL), layer),
        _resident((D_MODEL, D_MODEL), layer),
        _resident((D_PLE, D_MODEL), layer),
    ]


def _scratch(nseg, seg_len, pipelined):
    m = nseg * seg_len
    pre = [pltpu.VMEM((m, D_MODEL), bf16)] if pipelined else []
    return [
        pltpu.VMEM((nseg, HDR + seg_len, CONV_DIM), f32),
        pltpu.VMEM((nseg, HDR + seg_len, D_POOL), f32),
        pltpu.VMEM((SSM_GROUPS, D_STATE, GROUP_W), f32),
        pltpu.VMEM((m, D_SSM), f32),
        pltpu.VMEM((m, D_MODEL), f32),
        pltpu.VMEM((m, D_MODEL), bf16),
        pltpu.VMEM((m, max(hi - lo for lo, hi in FF_BLOCKS)), bf16),
        pltpu.VMEM((m, D_MODEL), f32),
        pltpu.VMEM((m, D_MODEL), f32),
        pltpu.VMEM((m, D_MODEL), f32),
        pltpu.VMEM((m, D_MODEL), f32),
    ] + pre


def _prompt_layer(layer, h, p_all, weights):
    nb, nl, _ = h.shape
    T = PROMPT_TILE
    assert nl % T == 0 and T % Q == 0
    nt = nl // T
    ntiles = nb * nt

    def tile_at(offset):
        def index(n):
            m = jnp.clip(n + offset, 0, ntiles - 1)
            return m // nt, m % nt
        return index

    nxt, cur, prev, prev2 = tile_at(1), tile_at(0), tile_at(-1), tile_at(-2)

    in_specs = [
        pl.BlockSpec((None, T, D_MODEL), lambda n: (*cur(n), 0)),
        pl.BlockSpec((None, T, D_MODEL), lambda n: (*nxt(n), 0)),
        pl.BlockSpec((None, None, T, D_PLE), lambda n: (layer, *prev(n), 0)),
    ] + _weight_specs(layer)
    out_shape = (
        jax.ShapeDtypeStruct((nb, nl, D_MODEL), f32),
        jax.ShapeDtypeStruct((nb, SSM_HEADS, SSM_HEAD_DIM, D_STATE), f32),
        jax.ShapeDtypeStruct((nb, 1, CONV_W - 1, CONV_DIM), f32),
        jax.ShapeDtypeStruct((nb, 1, POOL_HIST, D_POOL), f32),
    )
    out_specs = (
        pl.BlockSpec((None, T, D_MODEL), lambda n: (*prev2(n), 0)),
        pl.BlockSpec((None, SSM_HEADS, SSM_HEAD_DIM, D_STATE), lambda n: (cur(n)[0], 0, 0, 0)),
        pl.BlockSpec((None, 1, CONV_W - 1, CONV_DIM), lambda n: (cur(n)[0], 0, 0, 0)),
        pl.BlockSpec((None, 1, POOL_HIST, D_POOL), lambda n: (cur(n)[0], 0, 0, 0)),
    )
    h_new, ssm, conv, pool = pl.pallas_call(
        functools.partial(_prompt_kernel, tiles_per_seq=nt, tile=T),
        grid=(ntiles + 2,), in_specs=in_specs, out_specs=out_specs, out_shape=out_shape,
        scratch_shapes=_scratch(1, T, True),
        compiler_params=pltpu.CompilerParams(
            dimension_semantics=("arbitrary",), vmem_limit_bytes=VMEM_LIMIT),
        name=f"prompt_layer{layer}",
    )(h, h, p_all, *weights)
    return h_new, ssm, conv[:, 0], pool[:, 0]


def _sample_layer(layer, h2d, p_all, ssm_all, conv_all, pool_all, weights, nb, nl):
    assert nl % Q == 0
    M = nb * nl
    full = lambda *tail: pl.BlockSpec((None,) + tail, lambda i: (layer,) + (0,) * len(tail))
    in_specs = [
        pl.BlockSpec((M, D_MODEL), lambda i: (0, 0)),
        full(M, D_PLE),
        full(nb, SSM_HEADS, SSM_HEAD_DIM, D_STATE),
        full(nb, CONV_W - 1, CONV_DIM),
        full(nb, POOL_HIST, D_POOL),
    ] + _weight_specs(layer)
    out_shape = (
        jax.ShapeDtypeStruct((M, D_MODEL), f32),
        jax.ShapeDtypeStruct((nb, SSM_HEADS, SSM_HEAD_DIM, D_STATE), f32),
        jax.ShapeDtypeStruct((nb, CONV_W - 1, CONV_DIM), f32),
        jax.ShapeDtypeStruct((nb, POOL_HIST, D_POOL), f32),
    )
    out_specs = tuple(pl.BlockSpec(s.shape, lambda i, n=len(s.shape): (0,) * n) for s in out_shape)
    return pl.pallas_call(
        functools.partial(_sample_kernel, nseg=nb, seg_len=nl),
        grid=(1,), in_specs=in_specs, out_specs=out_specs, out_shape=out_shape,
        scratch_shapes=_scratch(nb, nl, False),
        compiler_params=pltpu.CompilerParams(
            dimension_semantics=("arbitrary",), vmem_limit_bytes=VMEM_LIMIT),
        name=f"sample_layer{layer}",
    )(h2d, p_all, ssm_all, conv_all, pool_all, *weights)


def _pack_weights(pre_mix_g, w_in, conv_w, conv_b, dt_bias, a_log, d_skip, ssm_norm_g, pool_w, pool_b,
                  pool_scale, w_out, post_mix_g, pre_ffn_g, w_gate, w_up, w_down, post_ffn_g, w_ple_gate,
                  w_ple_proj, ple_norm_g):
    dt_cols = D_SSM + CONV_DIM
    w_in = w_in.astype(bf16)
    w_in_ext = jnp.concatenate([
        w_in[:, :, 0:dt_cols],
        w_in[:, :, dt_cols + SSM_HEADS:],
        jnp.repeat(w_in[:, :, dt_cols:dt_cols + SSM_HEADS], SSM_HEAD_DIM, axis=2),
    ], axis=2)
    rep = lambda v: jnp.repeat(v, SSM_HEAD_DIM, axis=1)
    va = jnp.stack([pre_mix_g, post_mix_g, pre_ffn_g, post_ffn_g, ple_norm_g, conv_b]
                   + [conv_w[:, k] for k in range(CONV_W)], axis=1)
    va = jnp.pad(va, ((0, 0), (0, 16 - va.shape[1]), (0, 0)))
    vb = jnp.stack([rep(dt_bias), rep(a_log), rep(d_skip), ssm_norm_g,
                    pool_b.reshape(DEPTH, D_POOL), pool_scale], axis=1)
    vb = jnp.pad(vb, ((0, 0), (0, 8 - vb.shape[1]), (0, 0)))
    zeros = jnp.zeros((DEPTH, POOL_GROUP, POOL_GROUP), f32)
    pw = [jnp.concatenate([jnp.concatenate([pool_w[:, 2 * k], zeros], axis=2),
                           jnp.concatenate([zeros, pool_w[:, 2 * k + 1]], axis=2)], axis=1)
          for k in range(2)]
    pool_w2 = jnp.stack(pw, axis=1).astype(bf16)
    return (va, vb, w_in_ext, pool_w2, w_out.astype(bf16), w_gate.astype(bf16), w_up.astype(bf16),
            w_down.astype(bf16), w_ple_gate.astype(bf16), w_ple_proj.astype(bf16))


def kernel(x_prompt, x_sample, state_ssm, state_conv, state_pool, p_prompt, p_sample, pre_mix_g, w_in, conv_w, conv_b, dt_bias, a_log, d_skip, ssm_norm_g, pool_w, pool_b, pool_scale, w_out, post_mix_g, pre_ffn_g, w_gate, w_up, w_down, post_ffn_g, w_ple_gate, w_ple_proj, ple_norm_g):
    weights = _pack_weights(pre_mix_g, w_in, conv_w, conv_b, dt_bias, a_log, d_skip, ssm_norm_g, pool_w,
                            pool_b, pool_scale, w_out, post_mix_g, pre_ffn_g, w_gate, w_up, w_down,
                            post_ffn_g, w_ple_gate, w_ple_proj, ple_norm_g)
    nbs, nls, _ = x_sample.shape
    hp = x_prompt
    hs = x_sample.reshape(nbs * nls, D_MODEL)
    p_sample2d = p_sample.reshape(DEPTH, nbs * nls, D_PLE)
    outs_p, outs_s = [], []
    for i in range(DEPTH):
        hp, ssm, conv, pool = _prompt_layer(i, hp, p_prompt, weights)
        outs_p.append((ssm, conv, pool))
        hs, ssm, conv, pool = _sample_layer(i, hs, p_sample2d, state_ssm, state_conv, state_pool,
                                            weights, nbs, nls)
        outs_s.append((ssm, conv, pool))
    stack = lambda outs, j: jnp.stack([o[j] for o in outs])
    return (hp, hs.reshape(nbs, nls, D_MODEL),
            stack(outs_p, 0), stack(outs_p, 1), stack(outs_p, 2),
            stack(outs_s, 0), stack(outs_s, 1), stack(outs_s, 2))
```
